```python
import math
import jax, jax.numpy as jnp
from jax import lax
import numpy as np

D_MODEL = 1024
BATCH = 16
SEQ = 2048
DEPTH = 1

HEAD_DIM = 64
SB_HEADS = 8
SG_GROUPS = 8
SB_WIDTH = SB_HEADS * HEAD_DIM
SG_WIDTH = SG_GROUPS * HEAD_DIM
MIX_WIDTH = SB_WIDTH + SG_WIDTH
IN_WIDTH = 3 * SB_WIDTH + 2 * SG_WIDTH
Q_BLOCK = 128
CHUNK = 128
D_FF = 4 * D_MODEL
EPS = 1e-6

kernel_name = "hybrid_stickbreak_spatialgate_block"


def rms_norm(x, g):
    x32 = x.astype(jnp.float32)
    r = x32 * lax.rsqrt(jnp.mean(x32 * x32, axis=-1, keepdims=True) + EPS)
    return (r * g.astype(jnp.float32)).astype(x.dtype)


def stick_breaking_attention(q, k, v):
    B, H, S, Dh = q.shape
    scale = 1.0 / math.sqrt(Dh)
    outs = []
    for i in range(S // Q_BLOCK):
        k_end = (i + 1) * Q_BLOCK
        qb = q[:, :, i * Q_BLOCK:k_end]
        kb = k[:, :, :k_end]
        vb = v[:, :, :k_end]
        z = jnp.einsum('bhtd,bhsd->bhts', qb, kb).astype(jnp.float32) * scale
        t_pos = i * Q_BLOCK + jnp.arange(Q_BLOCK)[:, None]
        s_pos = jnp.arange(k_end)[None, :]
        mask = s_pos < t_pos
        log1m = jnp.where(mask, jax.nn.log_sigmoid(-z), 0.0)
        excl = lax.cumsum(log1m, axis=3, reverse=True) - log1m
        log_a = jax.nn.log_sigmoid(z) + excl
        a = jnp.where(mask, jnp.exp(log_a), 0.0)
        outs.append(jnp.einsum('bhts,bhsd->bhtd', a.astype(vb.dtype), vb))
    return jnp.concatenate(outs, axis=2)


def spatial_gating(u, v, w_s, b_s, v_norm_g):
    B, S, G, Dh = v.shape
    v = rms_norm(v, v_norm_g)
    vc = v.reshape(B, S // CHUNK, CHUNK, G, Dh)
    causal = jnp.tril(jnp.ones((CHUNK, CHUNK), dtype=bool))
    w = jnp.where(causal[None], w_s, 0.0).astype(v.dtype)
    y = jnp.einsum('gts,bcsgd->bctgd', w, vc)
    y = y + jnp.transpose(b_s)[None, None, :, :, None].astype(v.dtype)
    return u * y.reshape(B, S, G, Dh)


def setup_inputs(seed: int = 0) -> dict:
    key = jax.random.key(seed)
    ks = jax.random.split(key, 20)
    f = jnp.float32
    n = lambda k, shape, s: jax.random.normal(k, shape, f) * s
    inp = {
        "x": n(ks[0], (BATCH, SEQ, D_MODEL), 1.0),
        "norm1_g": 1.0 + n(ks[1], (D_MODEL,), 0.02),
        "w_in": n(ks[2], (D_MODEL, IN_WIDTH), D_MODEL ** -0.5),
        "q_norm_g": 1.0 + n(ks[3], (HEAD_DIM,), 0.02),
        "k_norm_g": 1.0 + n(ks[4], (HEAD_DIM,), 0.02),
        "sg_v_norm_g": 1.0 + n(ks[5], (HEAD_DIM,), 0.02),
        "sg_w": n(ks[6], (SG_GROUPS, CHUNK, CHUNK), CHUNK ** -0.5),
        "sg_b": 1.0 + n(ks[7], (SG_GROUPS, CHUNK), 0.01),
        "sb_out_norm_g": 1.0 + n(ks[8], (HEAD_DIM,), 0.02),
        "sg_out_norm_g": 1.0 + n(ks[9], (HEAD_DIM,), 0.02),
        "w_out": n(ks[10], (MIX_WIDTH, D_MODEL), MIX_WIDTH ** -0.5),
        "norm2_g": 1.0 + n(ks[11], (D_MODEL,), 0.02),
        "w_ff1": n(ks[12], (D_MODEL, D_FF), D_MODEL ** -0.5),
        "w_ff2": n(ks[13], (D_FF, D_MODEL), D_FF ** -0.5),
    }
    return inp


def reference(x, norm1_g, w_in, q_norm_g, k_norm_g, sg_v_norm_g, sg_w, sg_b,
              sb_out_norm_g, sg_out_norm_g, w_out, norm2_g, w_ff1, w_ff2):
    B, S, _ = x.shape
    h = x
    for _layer in range(DEPTH):
        xn = rms_norm(h, norm1_g)
        proj = jnp.einsum('bsd,de->bse', xn, w_in)
        q, k, v, u_sg, v_sg = jnp.split(
            proj, np.cumsum([SB_WIDTH, SB_WIDTH, SB_WIDTH, SG_WIDTH]).tolist(), axis=-1)
        q = rms_norm(q.reshape(B, S, SB_HEADS, HEAD_DIM), q_norm_g)
        k = rms_norm(k.reshape(B, S, SB_HEADS, HEAD_DIM), k_norm_g)
        v = v.reshape(B, S, SB_HEADS, HEAD_DIM)
        o_sb = stick_breaking_attention(q.transpose(0, 2, 1, 3), k.transpose(0, 2, 1, 3),
                                        v.transpose(0, 2, 1, 3)).transpose(0, 2, 1, 3)
        o_sb = rms_norm(o_sb, sb_out_norm_g)
        o_sg = spatial_gating(u_sg.reshape(B, S, SG_GROUPS, HEAD_DIM),
                              v_sg.reshape(B, S, SG_GROUPS, HEAD_DIM),
                              sg_w, sg_b, sg_v_norm_g)
        o_sg = rms_norm(o_sg, sg_out_norm_g)
        mix = jnp.concatenate([o_sb.reshape(B, S, SB_WIDTH),
                               o_sg.reshape(B, S, SG_WIDTH)], axis=-1)
        h = h + jnp.einsum('bse,ed->bsd', mix, w_out)
        hn = rms_norm(h, norm2_g)
        a = jax.nn.relu(jnp.einsum('bsd,df->bsf', hn, w_ff1))
        h = h + jnp.einsum('bsf,fd->bsd', a * a, w_ff2)
    return h
```

```python
import functools
import math

import jax
import jax.numpy as jnp
from jax import lax
from jax.experimental import pallas as pl
from jax.experimental.pallas import tpu as pltpu

D_MODEL = 1024
HEAD_DIM = 64
SB_HEADS = 8
SG_GROUPS = 8
SB_WIDTH = SB_HEADS * HEAD_DIM
SG_WIDTH = SG_GROUPS * HEAD_DIM
IN_WIDTH = 3 * SB_WIDTH + 2 * SG_WIDTH
CHUNK = 128
D_FF = 4 * D_MODEL
EPS = 1e-6

LANES = 128
PAIRS = SB_WIDTH // LANES
TQ = 256
TK = 128
LOG2E = 1.4426950408889634
VMEM_LIMIT = 56 * 1024 * 1024

f32 = jnp.float32
bf16 = jnp.bfloat16


def _head_rms(t, g, h0):
    sq = t * t
    s0 = jnp.sum(jnp.where(h0, sq, 0.0), axis=-1, keepdims=True)
    s1 = jnp.sum(jnp.where(h0, 0.0, sq), axis=-1, keepdims=True)
    ms = jnp.where(h0, s0, s1) * (1.0 / HEAD_DIM)
    return t * lax.rsqrt(ms + EPS) * g


def _in_proj_kernel(x_ref, g_ref, w_ref, o_ref):
    x = x_ref[...]
    ms = jnp.mean(x * x, axis=-1, keepdims=True)
    xn = (x * lax.rsqrt(ms + EPS) * g_ref[...]).astype(bf16)
    o_ref[...] = jnp.dot(xn, w_ref[...], preferred_element_type=f32)


def _in_proj(x2, g, w_bf, tm):
    m = x2.shape[0]
    return pl.pallas_call(
        _in_proj_kernel,
        name="in_proj",
        grid=(m // tm,),
        in_specs=[
            pl.BlockSpec((tm, D_MODEL), lambda i: (i, 0)),
            pl.BlockSpec((1, D_MODEL), lambda i: (0, 0)),
            pl.BlockSpec((D_MODEL, IN_WIDTH), lambda i: (0, 0)),
        ],
        out_specs=pl.BlockSpec((tm, IN_WIDTH), lambda i: (i, 0)),
        out_shape=jax.ShapeDtypeStruct((m, IN_WIDTH), f32),
        compiler_params=pltpu.CompilerParams(
            dimension_semantics=("arbitrary",), vmem_limit_bytes=VMEM_LIMIT),
    )(x2, g, w_bf)


def _sb_attn_kernel(q_ref, k_ref, v_ref, gq_ref, gk_ref, go_ref, tri_ref, o_ref,
                    k_s, qT_s, vT_s):
    seq = q_ref.shape[1]
    nqb = seq // TQ
    h0 = lax.broadcasted_iota(jnp.int32, (1, LANES), 1) < HEAD_DIM
    sub0 = lax.broadcasted_iota(jnp.int32, (LANES, 1), 0) < HEAD_DIM
    gq = gq_ref[...]
    gk = gk_ref[...]
    go = go_ref[...]
    scale2 = LOG2E / math.sqrt(HEAD_DIM)

    for i in range(nqb):
        rows = slice(i * TQ, (i + 1) * TQ)
        qT = (_head_rms(q_ref[0, rows, :], gq, h0) * scale2).T
        qT_s[0, i] = jnp.where(sub0, qT, 0.0).astype(bf16)
        qT_s[1, i] = jnp.where(sub0, 0.0, qT).astype(bf16)
        k_s[rows, :] = _head_rms(k_ref[0, rows, :], gk, h0).astype(bf16)
        vT = v_ref[0, rows, :].T
        for j in range(TQ // TK):
            vt = vT[:, j * TK:(j + 1) * TK]
            vT_s[0, i * (TQ // TK) + j] = jnp.where(sub0, vt, 0.0).astype(bf16)
            vT_s[1, i * (TQ // TK) + j] = jnp.where(sub0, 0.0, vt).astype(bf16)

    def tile(qb, kt, carry, masked):
        c0, c1, acc = carry
        k_t = k_s[pl.ds(pl.multiple_of(kt * TK, TK), TK), :]
        if masked:
            s_idx = kt * TK + lax.broadcasted_iota(jnp.int32, (TK, TQ), 0)
            t_idx = qb * TQ + lax.broadcasted_iota(jnp.int32, (TK, TQ), 1)
            mask = s_idx < t_idx
        probs = []
        new_c = []
        for h, c in ((0, c0), (1, c1)):
            z2 = jnp.dot(k_t, qT_s[h, qb], preferred_element_type=f32)
            w = jnp.exp2(-jnp.abs(z2))
            sp = jnp.maximum(z2, 0.0) + jnp.log(1.0 + w) * LOG2E
            if masked:
                sp = jnp.where(mask, sp, 0.0)
            hi = sp.astype(bf16)
            lo = (sp - hi.astype(f32)).astype(bf16)
            csum = jnp.dot(tri_ref[...], jnp.concatenate([hi, lo], axis=0),
                           preferred_element_type=f32) + c
            a = jnp.exp2(z2 - csum)
            if masked:
                a = jnp.where(mask, a, 0.0)
            probs.append(a.astype(bf16))
            new_c.append(csum[0:1, :])
        v_cat = jnp.concatenate([vT_s[0, kt], vT_s[1, kt]], axis=1)
        acc = acc + jnp.dot(v_cat, jnp.concatenate(probs, axis=0),
                            preferred_element_type=f32)
        return new_c[0], new_c[1], acc

    def q_block(qb, _):
        carry = (jnp.zeros((1, TQ), f32), jnp.zeros((1, TQ), f32), jnp.zeros((LANES, TQ), f32))
        nkt = (qb + 1) * (TQ // TK)
        for d in range(TQ // TK):
            carry = tile(qb, nkt - 1 - d, carry, True)
        carry = lax.fori_loop(
            0, qb * (TQ // TK),
            lambda i, cr: tile(qb, qb * (TQ // TK) - 1 - i, cr, False), carry)
        out = _head_rms(carry[2].T, go, h0)
        o_ref[0, pl.ds(pl.multiple_of(qb * TQ, TQ), TQ), :] = out.astype(o_ref.dtype)
        return 0

    lax.fori_loop(0, nqb, q_block, 0)


def _sb_attn(proj3, gq, gk, go, tri):
    b, seq, _ = proj3.shape
    blk = lambda off: pl.BlockSpec((1, seq, LANES), lambda i, p: (i, 0, off + p))
    vec = pl.BlockSpec((1, LANES), lambda i, p: (0, 0))
    return pl.pallas_call(
        _sb_attn_kernel,
        name="sb_attn",
        grid=(b, PAIRS),
        in_specs=[blk(0), blk(PAIRS), blk(2 * PAIRS), vec, vec, vec,
                  pl.BlockSpec((TK, 2 * TK), lambda i, p: (0, 0))],
        out_specs=pl.BlockSpec((1, seq, LANES), lambda i, p: (i, 0, p)),
        out_shape=jax.ShapeDtypeStruct((b, seq, SB_WIDTH), bf16),
        scratch_shapes=[
            pltpu.VMEM((seq, LANES), bf16),
            pltpu.VMEM((2, seq // TQ, LANES, TQ), bf16),
            pltpu.VMEM((2, seq // TK, LANES, TK), bf16),
        ],
        compiler_params=pltpu.CompilerParams(
            dimension_semantics=("arbitrary", "arbitrary"), vmem_limit_bytes=VMEM_LIMIT),
    )(proj3, proj3, proj3, gq, gk, go, tri)


def _sg_kernel(u_ref, v_ref, w_ref, b_ref, gv_ref, go_ref, o_ref):
    seq = u_ref.shape[1]
    h0 = lax.broadcasted_iota(jnp.int32, (1, LANES), 1) < HEAD_DIM
    causal = (lax.broadcasted_iota(jnp.int32, (CHUNK, CHUNK), 1)
              <= lax.broadcasted_iota(jnp.int32, (CHUNK, CHUNK), 0))
    w_cat = jnp.concatenate(
        [jnp.where(causal, w_ref[0], 0.0), jnp.where(causal, w_ref[1], 0.0)], axis=1).astype(bf16)
    bias = b_ref[...]
    gv = gv_ref[...]
    go = go_ref[...]

    def chunk(c, _):
        rows = pl.ds(pl.multiple_of(c * CHUNK, CHUNK), CHUNK)
        vn = _head_rms(v_ref[0, rows, :], gv, h0)
        rhs = jnp.concatenate([jnp.where(h0, vn, 0.0), jnp.where(h0, 0.0, vn)], axis=0).astype(bf16)
        y = jnp.dot(w_cat, rhs, preferred_element_type=f32) + bias
        o = _head_rms(u_ref[0, rows, :] * y, go, h0)
        o_ref[0, rows, :] = o.astype(o_ref.dtype)
        return 0

    lax.fori_loop(0, seq // CHUNK, chunk, 0)


def _spatial_gate(proj3, sg_w, bias_tiles, gv, go):
    b, seq, _ = proj3.shape
    blk = lambda off: pl.BlockSpec((1, seq, LANES), lambda i, p: (i, 0, off + p))
    vec = pl.BlockSpec((1, LANES), lambda i, p: (0, 0))
    return pl.pallas_call(
        _sg_kernel,
        name="spatial_gate",
        grid=(b, PAIRS),
        in_specs=[blk(3 * PAIRS), blk(4 * PAIRS),
                  pl.BlockSpec((2, CHUNK, CHUNK), lambda i, p: (p, 0, 0)),
                  pl.BlockSpec((CHUNK, LANES), lambda i, p: (0, p)),
                  vec, vec],
        out_specs=pl.BlockSpec((1, seq, LANES), lambda i, p: (i, 0, p)),
        out_shape=jax.ShapeDtypeStruct((b, seq, SG_WIDTH), bf16),
        compiler_params=pltpu.CompilerParams(
            dimension_semantics=("arbitrary", "arbitrary"), vmem_limit_bytes=VMEM_LIMIT),
    )(proj3, proj3, sg_w, bias_tiles, gv, go)


def _mlp_kernel(x_ref, sb_ref, sg_ref, wo_ref, g2_ref, w1_ref, w2_ref, o_ref, *, ff_chunk):
    mix = jnp.concatenate([sb_ref[...], sg_ref[...]], axis=1)
    h1 = x_ref[...] + jnp.dot(mix, wo_ref[...], preferred_element_type=f32)
    ms = jnp.mean(h1 * h1, axis=-1, keepdims=True)
    hn = (h1 * lax.rsqrt(ms + EPS) * g2_ref[...]).astype(bf16)
    ff = jnp.zeros_like(h1)
    for j in range(D_FF // ff_chunk):
        cols = slice(j * ff_chunk, (j + 1) * ff_chunk)
        a = jnp.maximum(jnp.dot(hn, w1_ref[:, cols], preferred_element_type=f32), 0.0)
        ff = ff + jnp.dot((a * a).astype(bf16), w2_ref[cols, :], preferred_element_type=f32)
    o_ref[...] = h1 + ff


def _mlp(x2, o_sb, o_sg, wo_bf, g2, w1_bf, w2_bf, tm, ff_chunk):
    m = x2.shape[0]
    const = lambda shape: pl.BlockSpec(shape, lambda i: (0, 0), pipeline_mode=pl.Buffered(1))
    return pl.pallas_call(
        functools.partial(_mlp_kernel, ff_chunk=ff_chunk),
        name="out_proj_mlp",
        grid=(m // tm,),
        in_specs=[
            pl.BlockSpec((tm, D_MODEL), lambda i: (i, 0)),
            pl.BlockSpec((tm, SB_WIDTH), lambda i: (i, 0)),
            pl.BlockSpec((tm, SG_WIDTH), lambda i: (i, 0)),
            const((D_MODEL, D_MODEL)),
            const((1, D_MODEL)),
            const((D_MODEL, D_FF)),
            const((D_FF, D_MODEL)),
        ],
        out_specs=pl.BlockSpec((tm, D_MODEL), lambda i: (i, 0)),
        out_shape=jax.ShapeDtypeStruct((m, D_MODEL), f32),
        compiler_params=pltpu.CompilerParams(
            dimension_semantics=("arbitrary",), vmem_limit_bytes=VMEM_LIMIT),
    )(x2, o_sb, o_sg, wo_bf, g2, w1_bf, w2_bf)


def kernel(x, norm1_g, w_in, q_norm_g, k_norm_g, sg_v_norm_g, sg_w, sg_b, sb_out_norm_g,
           sg_out_norm_g, w_out, norm2_g, w_ff1, w_ff2):
    b, seq, d = x.shape
    assert d == D_MODEL and seq % TQ == 0 and w_in.shape == (D_MODEL, IN_WIDTH)
    m = b * seq
    x2 = x.reshape(m, d)
    pair_gain = lambda g: jnp.tile(g.astype(f32), 2).reshape(1, LANES)

    proj = _in_proj(x2, norm1_g.reshape(1, d), w_in.astype(bf16), tm=512)
    proj3 = proj.reshape(b, seq, IN_WIDTH)

    tri = (jnp.arange(TK)[None, :] >= jnp.arange(TK)[:, None]).astype(bf16)
    tri2 = jnp.concatenate([tri, tri], axis=1)
    o_sb = _sb_attn(proj3, pair_gain(q_norm_g), pair_gain(k_norm_g), pair_gain(sb_out_norm_g), tri2)

    bias_tiles = jnp.repeat(sg_b.astype(f32).T, HEAD_DIM, axis=1)
    o_sg = _spatial_gate(proj3, sg_w, bias_tiles, pair_gain(sg_v_norm_g), pair_gain(sg_out_norm_g))

    out = _mlp(x2, o_sb.reshape(m, SB_WIDTH), o_sg.reshape(m, SG_WIDTH), w_out.astype(bf16),
               norm2_g.reshape(1, d), w_ff1.astype(bf16), w_ff2.astype(bf16), tm=512, ff_chunk=1024)
    return out.reshape(b, seq, d)
```

```python
import functools
import math

import jax
import jax.numpy as jnp
from jax import lax
from jax.experimental import pallas as pl
from jax.experimental.pallas import tpu as pltpu

D_MODEL = 1024
HEAD_DIM = 64
SB_HEADS = 8
SG_GROUPS = 8
SB_WIDTH = SB_HEADS * HEAD_DIM
SG_WIDTH = SG_GROUPS * HEAD_DIM
IN_WIDTH = 3 * SB_WIDTH + 2 * SG_WIDTH
CHUNK = 128
D_FF = 4 * D_MODEL
EPS = 1e-6

LANES = 128
PAIRS = SB_WIDTH // LANES
TQ = 512
TK = 128
KT_PER_QB = TQ // TK
UNROLL = 4
PIPE_DEPTH = 3
LOG2E = 1.4426950408889634
VMEM_LIMIT = 56 * 1024 * 1024

f32 = jnp.float32
bf16 = jnp.bfloat16


def _head_rms(t, g, h0):
    sq = t * t
    s0 = jnp.sum(jnp.where(h0, sq, 0.0), axis=-1, keepdims=True)
    s1 = jnp.sum(jnp.where(h0, 0.0, sq), axis=-1, keepdims=True)
    ms = jnp.where(h0, s0, s1) * (1.0 / HEAD_DIM)
    return t * lax.rsqrt(ms + EPS) * g


def _in_proj_kernel(x_ref, g_ref, w_ref, o_ref):
    x = x_ref[...]
    ms = jnp.mean(x * x, axis=-1, keepdims=True)
    xn = (x * lax.rsqrt(ms + EPS) * g_ref[...]).astype(bf16)
    o_ref[...] = jnp.dot(xn, w_ref[...], preferred_element_type=f32)


def _in_proj(x2, g, w_bf, tm):
    m = x2.shape[0]
    return pl.pallas_call(
        _in_proj_kernel,
        name="in_proj",
        grid=(m // tm,),
        in_specs=[
            pl.BlockSpec((tm, D_MODEL), lambda i: (i, 0)),
            pl.BlockSpec((1, D_MODEL), lambda i: (0, 0)),
            pl.BlockSpec((D_MODEL, IN_WIDTH), lambda i: (0, 0)),
        ],
        out_specs=pl.BlockSpec((tm, IN_WIDTH), lambda i: (i, 0)),
        out_shape=jax.ShapeDtypeStruct((m, IN_WIDTH), f32),
        compiler_params=pltpu.CompilerParams(
            dimension_semantics=("arbitrary",), vmem_limit_bytes=VMEM_LIMIT),
    )(x2, g, w_bf)


def _sb_attn_kernel(q_ref, k_ref, v_ref, gq_ref, gk_ref, go_ref, tri_ref, o_ref,
                    k_s, qT_s, vT_s, acc_s):
    seq = q_ref.shape[1]
    nqb = seq // TQ
    h0 = lax.broadcasted_iota(jnp.int32, (1, LANES), 1) < HEAD_DIM
    sub0 = lax.broadcasted_iota(jnp.int32, (LANES, 1), 0) < HEAD_DIM
    gq = gq_ref[...]
    gk = gk_ref[...]
    go = go_ref[...]
    scale2 = LOG2E / math.sqrt(HEAD_DIM)

    for i in range(nqb):
        rows = slice(i * TQ, (i + 1) * TQ)
        qT = (_head_rms(q_ref[0, rows, :], gq, h0) * scale2).T
        qT_s[0, i] = jnp.where(sub0, qT, 0.0).astype(bf16)
        qT_s[1, i] = jnp.where(sub0, 0.0, qT).astype(bf16)
        k_s[rows, :] = _head_rms(k_ref[0, rows, :], gk, h0).astype(bf16)
        vT = v_ref[0, rows, :].T
        for j in range(TQ // TK):
            vt = vT[:, j * TK:(j + 1) * TK]
            vT_s[0, i * (TQ // TK) + j] = jnp.where(sub0, vt, 0.0).astype(bf16)
            vT_s[1, i * (TQ // TK) + j] = jnp.where(sub0, 0.0, vt).astype(bf16)

    def scores(k_t, qT, mask):
        z2 = jnp.dot(k_t, qT, preferred_element_type=f32)
        w = jnp.exp2(-jnp.abs(z2))
        sp = jnp.maximum(z2, 0.0) + jnp.log(1.0 + w) * LOG2E
        if mask is not None:
            sp = jnp.where(mask, sp, 0.0)
        hi = sp.astype(bf16)
        lo = (sp - hi.astype(f32)).astype(bf16)
        return z2, jnp.concatenate([hi, lo], axis=0)

    def weights(z2, hi_lo, c, mask):
        csum = jnp.dot(tri_ref[...], hi_lo, preferred_element_type=f32) + c
        a = jnp.exp2(z2 - csum)
        if mask is not None:
            a = jnp.where(mask, a, 0.0)
        return a.astype(bf16), csum[0:1, :]

    def key_rows(kt):
        return k_s[pl.ds(pl.multiple_of(kt * TK, TK), TK), :]

    def run_tiles(qb, tiles, c):
        chains = [(t, h) for t in range(len(tiles)) for h in (0, 1)]
        in_flight = []
        probs = {}

        def finish(entry):
            t, h, z2, hi_lo = entry
            kt, lane, mask = tiles[t]
            a, new = weights(z2, hi_lo, c[h][:, lane:], mask)
            c[h] = jnp.concatenate([c[h][:, :lane], new], axis=1) if lane else new
            probs[(t, h)] = a
            if h == 1:
                v_cat = jnp.concatenate([vT_s[0, kt], vT_s[1, kt]], axis=1)
                acc_s[:, lane:] += jnp.dot(
                    v_cat, jnp.concatenate([probs.pop((t, 0)), probs.pop((t, 1))], axis=0),
                    preferred_element_type=f32)

        for t, h in chains:
            kt, lane, mask = tiles[t]
            in_flight.append((t, h) + scores(key_rows(kt), qT_s[h, qb, :, lane:], mask))
            if len(in_flight) > PIPE_DEPTH:
                finish(in_flight.pop(0))
        while in_flight:
            finish(in_flight.pop(0))
        return c

    def diagonal_tiles(qb, c):
        tiles = []
        for j in reversed(range(KT_PER_QB)):
            width = TQ - j * TK
            mask = (lax.broadcasted_iota(jnp.int32, (TK, width), 0)
                    < lax.broadcasted_iota(jnp.int32, (TK, width), 1))
            tiles.append((qb * KT_PER_QB + j, j * TK, mask))
        return run_tiles(qb, tiles, c)

    def full_tiles(qb, kt_top, c0, c1):
        c = run_tiles(qb, [(kt_top - u, 0, None) for u in range(UNROLL)], [c0, c1])
        return c[0], c[1]

    def q_block(qb, _):
        acc_s[...] = jnp.zeros_like(acc_s)
        c0, c1 = diagonal_tiles(qb, [jnp.zeros((1, TQ), f32), jnp.zeros((1, TQ), f32)])
        c0, c1 = lax.fori_loop(
            0, qb * (KT_PER_QB // UNROLL),
            lambda i, c: full_tiles(qb, qb * KT_PER_QB - 1 - i * UNROLL, *c), (c0, c1))
        out = _head_rms(acc_s[...].T, go, h0)
        o_ref[0, pl.ds(pl.multiple_of(qb * TQ, TQ), TQ), :] = out.astype(o_ref.dtype)
        return 0

    lax.fori_loop(0, nqb, q_block, 0)


def _sb_attn(proj3, gq, gk, go, tri):
    b, seq, _ = proj3.shape
    blk = lambda off: pl.BlockSpec((1, seq, LANES), lambda i, p: (i, 0, off + p))
    vec = pl.BlockSpec((1, LANES), lambda i, p: (0, 0))
    return pl.pallas_call(
        _sb_attn_kernel,
        name="sb_attn",
        grid=(b, PAIRS),
        in_specs=[blk(0), blk(PAIRS), blk(2 * PAIRS), vec, vec, vec,
                  pl.BlockSpec((TK, 2 * TK), lambda i, p: (0, 0))],
        out_specs=pl.BlockSpec((1, seq, LANES), lambda i, p: (i, 0, p)),
        out_shape=jax.ShapeDtypeStruct((b, seq, SB_WIDTH), bf16),
        scratch_shapes=[
            pltpu.VMEM((seq, LANES), bf16),
            pltpu.VMEM((2, seq // TQ, LANES, TQ), bf16),
            pltpu.VMEM((2, seq // TK, LANES, TK), bf16),
            pltpu.VMEM((LANES, TQ), f32),
        ],
        compiler_params=pltpu.CompilerParams(
            dimension_semantics=("arbitrary", "arbitrary"), vmem_limit_bytes=VMEM_LIMIT),
    )(proj3, proj3, proj3, gq, gk, go, tri)


def _sg_kernel(u_ref, v_ref, w_ref, b_ref, gv_ref, go_ref, o_ref):
    seq = u_ref.shape[1]
    h0 = lax.broadcasted_iota(jnp.int32, (1, LANES), 1) < HEAD_DIM
    causal = (lax.broadcasted_iota(jnp.int32, (CHUNK, CHUNK), 1)
              <= lax.broadcasted_iota(jnp.int32, (CHUNK, CHUNK), 0))
    w_cat = jnp.concatenate(
        [jnp.where(causal, w_ref[0], 0.0), jnp.where(causal, w_ref[1], 0.0)], axis=1).astype(bf16)
    bias = b_ref[...]
    gv = gv_ref[...]
    go = go_ref[...]

    def chunk(c, _):
        rows = pl.ds(pl.multiple_of(c * CHUNK, CHUNK), CHUNK)
        vn = _head_rms(v_ref[0, rows, :], gv, h0)
        rhs = jnp.concatenate([jnp.where(h0, vn, 0.0), jnp.where(h0, 0.0, vn)], axis=0).astype(bf16)
        y = jnp.dot(w_cat, rhs, preferred_element_type=f32) + bias
        o = _head_rms(u_ref[0, rows, :] * y, go, h0)
        o_ref[0, rows, :] = o.astype(o_ref.dtype)
        return 0

    lax.fori_loop(0, seq // CHUNK, chunk, 0)


def _spatial_gate(proj3, sg_w, bias_tiles, gv, go):
    b, seq, _ = proj3.shape
    blk = lambda off: pl.BlockSpec((1, seq, LANES), lambda i, p: (i, 0, off + p))
    vec = pl.BlockSpec((1, LANES), lambda i, p: (0, 0))
    return pl.pallas_call(
        _sg_kernel,
        name="spatial_gate",
        grid=(b, PAIRS),
        in_specs=[blk(3 * PAIRS), blk(4 * PAIRS),
                  pl.BlockSpec((2, CHUNK, CHUNK), lambda i, p: (p, 0, 0)),
                  pl.BlockSpec((CHUNK, LANES), lambda i, p: (0, p)),
                  vec, vec],
        out_specs=pl.BlockSpec((1, seq, LANES), lambda i, p: (i, 0, p)),
        out_shape=jax.ShapeDtypeStruct((b, seq, SG_WIDTH), bf16),
        compiler_params=pltpu.CompilerParams(
            dimension_semantics=("arbitrary", "arbitrary"), vmem_limit_bytes=VMEM_LIMIT),
    )(proj3, proj3, sg_w, bias_tiles, gv, go)


def _mlp_kernel(x_ref, sb_ref, sg_ref, wo_ref, g2_ref, w1_ref, w2_ref, o_ref, *, ff_chunk):
    mix = jnp.concatenate([sb_ref[...], sg_ref[...]], axis=1)
    h1 = x_ref[...] + jnp.dot(mix, wo_ref[...], preferred_element_type=f32)
    ms = jnp.mean(h1 * h1, axis=-1, keepdims=True)
    hn = (h1 * lax.rsqrt(ms + EPS) * g2_ref[...]).astype(bf16)
    ff = jnp.zeros_like(h1)
    for j in range(D_FF // ff_chunk):
        cols = slice(j * ff_chunk, (j + 1) * ff_chunk)
        a = jnp.maximum(jnp.dot(hn, w1_ref[:, cols], preferred_element_type=f32), 0.0)
        ff = ff + jnp.dot((a * a).astype(bf16), w2_ref[cols, :], preferred_element_type=f32)
    o_ref[...] = h1 + ff


def _mlp(x2, o_sb, o_sg, wo_bf, g2, w1_bf, w2_bf, tm, ff_chunk):
    m = x2.shape[0]
    const = lambda shape: pl.BlockSpec(shape, lambda i: (0, 0), pipeline_mode=pl.Buffered(1))
    return pl.pallas_call(
        functools.partial(_mlp_kernel, ff_chunk=ff_chunk),
        name="out_proj_mlp",
        grid=(m // tm,),
        in_specs=[
            pl.BlockSpec((tm, D_MODEL), lambda i: (i, 0)),
            pl.BlockSpec((tm, SB_WIDTH), lambda i: (i, 0)),
            pl.BlockSpec((tm, SG_WIDTH), lambda i: (i, 0)),
            const((D_MODEL, D_MODEL)),
            const((1, D_MODEL)),
            const((D_MODEL, D_FF)),
            const((D_FF, D_MODEL)),
        ],
        out_specs=pl.BlockSpec((tm, D_MODEL), lambda i: (i, 0)),
        out_shape=jax.ShapeDtypeStruct((m, D_MODEL), f32),
        compiler_params=pltpu.CompilerParams(
            dimension_semantics=("arbitrary",), vmem_limit_bytes=VMEM_LIMIT),
    )(x2, o_sb, o_sg, wo_bf, g2, w1_bf, w2_bf)


def kernel(x, norm1_g, w_in, q_norm_g, k_norm_g, sg_v_norm_g, sg_w, sg_b, sb_out_norm_g,
           sg_out_norm_g, w_out, norm2_g, w_ff1, w_ff2):
    b, seq, d = x.shape
    assert d == D_MODEL and seq % TQ == 0 and w_in.shape == (D_MODEL, IN_WIDTH)
    m = b * seq
    x2 = x.reshape(m, d)
    pair_gain = lambda g: jnp.tile(g.astype(f32), 2).reshape(1, LANES)

    proj = _in_proj(x2, norm1_g.reshape(1, d), w_in.astype(bf16), tm=512)
    proj3 = proj.reshape(b, seq, IN_WIDTH)

    tri = (jnp.arange(TK)[None, :] >= jnp.arange(TK)[:, None]).astype(bf16)
    tri2 = jnp.concatenate([tri, tri], axis=1)
    o_sb = _sb_attn(proj3, pair_gain(q_norm_g), pair_gain(k_norm_g), pair_gain(sb_out_norm_g), tri2)

    bias_tiles = jnp.repeat(sg_b.astype(f32).T, HEAD_DIM, axis=1)
    o_sg = _spatial_gate(proj3, sg_w, bias_tiles, pair_gain(sg_v_norm_g), pair_gain(sg_out_norm_g))

    out = _mlp(x2, o_sb.reshape(m, SB_WIDTH), o_sg.reshape(m, SG_WIDTH), w_out.astype(bf16),
               norm2_g.reshape(1, d), w_ff1.astype(bf16), w_ff2.astype(bf16), tm=512, ff_chunk=1024)
    return out.reshape(b, seq, d)
```

```python
import functools
import math

import jax
import jax.numpy as jnp
from jax import lax
from jax.experimental import pallas as pl
from jax.experimental.pallas import tpu as pltpu

D_MODEL = 1024
HEAD_DIM = 64
SB_HEADS = 8
SG_GROUPS = 8
SB_WIDTH = SB_HEADS * HEAD_DIM
SG_WIDTH = SG_GROUPS * HEAD_DIM
IN_WIDTH = 3 * SB_WIDTH + 2 * SG_WIDTH
CHUNK = 128
D_FF = 4 * D_MODEL
EPS = 1e-6

LANES = 128
PAIRS = SB_WIDTH // LANES
TQ = 512
TK = 128
KT_PER_QB = TQ // TK
UNROLL = 4
PIPE_DEPTH = 3
LOG2E = 1.4426950408889634
VMEM_LIMIT = 56 * 1024 * 1024

f32 = jnp.float32
bf16 = jnp.bfloat16


def _head_rms(t, g, h0):
    sq = t * t
    s0 = jnp.sum(jnp.where(h0, sq, 0.0), axis=-1, keepdims=True)
    s1 = jnp.sum(jnp.where(h0, 0.0, sq), axis=-1, keepdims=True)
    ms = jnp.where(h0, s0, s1) * (1.0 / HEAD_DIM)
    return t * lax.rsqrt(ms + EPS) * g


def _in_proj_kernel(x_ref, g_ref, w_ref, gq_ref, gk_ref, qkv_ref, sg_ref, *, sub_rows):
    h0 = lax.broadcasted_iota(jnp.int32, (1, LANES), 1) < HEAD_DIM
    gq = gq_ref[...] * (LOG2E / math.sqrt(HEAD_DIM))
    gk = gk_ref[...]
    for r in range(x_ref.shape[0] // sub_rows):
        rows = slice(r * sub_rows, (r + 1) * sub_rows)
        x = x_ref[rows, :]
        ms = jnp.mean(x * x, axis=-1, keepdims=True)
        xn = (x * lax.rsqrt(ms + EPS) * g_ref[...]).astype(bf16)
        proj = jnp.dot(xn, w_ref[...], preferred_element_type=f32)
        for t in range(PAIRS):
            q_cols = slice(t * LANES, (t + 1) * LANES)
            k_cols = slice(SB_WIDTH + t * LANES, SB_WIDTH + (t + 1) * LANES)
            qkv_ref[rows, q_cols] = _head_rms(proj[:, q_cols], gq, h0).astype(bf16)
            qkv_ref[rows, k_cols] = _head_rms(proj[:, k_cols], gk, h0).astype(bf16)
        qkv_ref[rows, 2 * SB_WIDTH:] = proj[:, 2 * SB_WIDTH:3 * SB_WIDTH].astype(bf16)
        sg_ref[rows, :] = proj[:, 3 * SB_WIDTH:]


def _in_proj(x2, g, w_bf, gq, gk, tm, sub_rows):
    m = x2.shape[0]
    vec = pl.BlockSpec((1, LANES), lambda i: (0, 0))
    return pl.pallas_call(
        functools.partial(_in_proj_kernel, sub_rows=sub_rows),
        name="in_proj",
        grid=(m // tm,),
        in_specs=[
            pl.BlockSpec((tm, D_MODEL), lambda i: (i, 0)),
            pl.BlockSpec((1, D_MODEL), lambda i: (0, 0)),
            pl.BlockSpec((D_MODEL, IN_WIDTH), lambda i: (0, 0), pipeline_mode=pl.Buffered(1)),
            vec, vec,
        ],
        out_specs=[pl.BlockSpec((tm, 3 * SB_WIDTH), lambda i: (i, 0)),
                   pl.BlockSpec((tm, 2 * SG_WIDTH), lambda i: (i, 0))],
        out_shape=[jax.ShapeDtypeStruct((m, 3 * SB_WIDTH), bf16),
                   jax.ShapeDtypeStruct((m, 2 * SG_WIDTH), f32)],
        compiler_params=pltpu.CompilerParams(
            dimension_semantics=("arbitrary",), vmem_limit_bytes=VMEM_LIMIT),
    )(x2, g, w_bf, gq, gk)


def _sb_attn_kernel(q_ref, k_ref, v_ref, go_ref, tri_ref, o_ref, qT_s, vT_s, acc_s):
    seq = q_ref.shape[1]
    nqb = seq // TQ
    sub0 = lax.broadcasted_iota(jnp.int32, (LANES, 1), 0) < HEAD_DIM
    go = go_ref[...]
    sign_bit = jnp.uint32(0x80000000)

    for i in range(nqb):
        rows = slice(i * TQ, (i + 1) * TQ)
        qT = q_ref[0, rows, :].astype(f32).T
        qT_s[0, i] = jnp.where(sub0, qT, 0.0).astype(bf16)
        qT_s[1, i] = jnp.where(sub0, 0.0, qT).astype(bf16)
        vT = v_ref[0, rows, :].astype(f32).T
        for j in range(TQ // TK):
            vt = vT[:, j * TK:(j + 1) * TK]
            vT_s[0, i * (TQ // TK) + j] = jnp.where(sub0, vt, 0.0).astype(bf16)
            vT_s[1, i * (TQ // TK) + j] = jnp.where(sub0, 0.0, vt).astype(bf16)

    def scores(k_t, qT, mask):
        z2 = jnp.dot(k_t, qT, preferred_element_type=f32)
        neg_abs = pltpu.bitcast(pltpu.bitcast(z2, jnp.uint32) | sign_bit, f32)
        w = jnp.exp2(neg_abs)
        sp = jnp.maximum(z2, 0.0) + jnp.log(1.0 + w) * LOG2E
        if mask is not None:
            sp = jnp.where(mask, sp, 0.0)
        hi = sp.astype(bf16)
        lo = (sp - hi.astype(f32)).astype(bf16)
        return z2, jnp.concatenate([hi, lo], axis=0)

    def weights(z2, hi_lo, c, mask):
        csum = jnp.dot(tri_ref[...], hi_lo, preferred_element_type=f32) + c
        a = jnp.exp2(z2 - csum)
        if mask is not None:
            a = jnp.where(mask, a, 0.0)
        return a.astype(bf16), csum[0:1, :]

    def key_rows(kt):
        return k_ref[0, kt * TK:(kt + 1) * TK, :]

    def finish_block(qb):
        acc = acc_s[qb]
        sq = acc * acc
        r0 = lax.rsqrt(jnp.sum(sq[:HEAD_DIM], axis=0, keepdims=True) * (1.0 / HEAD_DIM) + EPS)
        r1 = lax.rsqrt(jnp.sum(sq[HEAD_DIM:], axis=0, keepdims=True) * (1.0 / HEAD_DIM) + EPS)
        out = (acc * jnp.where(sub0, r0, r1)).T * go
        o_ref[0, qb * TQ:(qb + 1) * TQ, :] = out.astype(o_ref.dtype)

    tiles = []
    for qb in range(nqb):
        for j in reversed(range(KT_PER_QB)):
            width = TQ - j * TK
            mask = (lax.broadcasted_iota(jnp.int32, (TK, width), 0)
                    < lax.broadcasted_iota(jnp.int32, (TK, width), 1))
            tiles.append((qb, qb * KT_PER_QB + j, j * TK, mask))
        tiles += [(qb, kt, 0, None) for kt in reversed(range(qb * KT_PER_QB))]
    last_tile = {qb: max(i for i, t in enumerate(tiles) if t[0] == qb) for qb in range(nqb)}

    carry = {}
    probs = {}

    def finish(i, h, z2, hi_lo):
        qb, kt, lane, mask = tiles[i]
        c = carry.get((qb, h), jnp.zeros((1, TQ), f32))
        a, new = weights(z2, hi_lo, c[:, lane:], mask)
        carry[(qb, h)] = jnp.concatenate([c[:, :lane], new], axis=1) if lane else new
        probs[h] = a
        if h == 1:
            v_cat = jnp.concatenate([vT_s[0, kt], vT_s[1, kt]], axis=1)
            acc_s[qb, :, lane:] += jnp.dot(v_cat, jnp.concatenate([probs[0], probs[1]], axis=0),
                                           preferred_element_type=f32)
            if i == last_tile[qb]:
                finish_block(qb)

    for qb in range(nqb):
        acc_s[qb] = jnp.zeros((LANES, TQ), f32)
    in_flight = []
    for i, (qb, kt, lane, mask) in enumerate(tiles):
        for h in (0, 1):
            in_flight.append((i, h) + scores(key_rows(kt), qT_s[h, qb, :, lane:], mask))
            if len(in_flight) > PIPE_DEPTH:
                finish(*in_flight.pop(0))
    while in_flight:
        finish(*in_flight.pop(0))


def _sb_attn(qkv3, go, tri):
    b, seq, _ = qkv3.shape
    blk = lambda off: pl.BlockSpec((1, seq, LANES), lambda i, p: (i, 0, off + p))
    return pl.pallas_call(
        _sb_attn_kernel,
        name="sb_attn",
        grid=(b, PAIRS),
        in_specs=[blk(0), blk(PAIRS), blk(2 * PAIRS),
                  pl.BlockSpec((1, LANES), lambda i, p: (0, 0)),
                  pl.BlockSpec((TK, 2 * TK), lambda i, p: (0, 0))],
        out_specs=pl.BlockSpec((1, seq, LANES), lambda i, p: (i, 0, p)),
        out_shape=jax.ShapeDtypeStruct((b, seq, SB_WIDTH), bf16),
        scratch_shapes=[
            pltpu.VMEM((2, seq // TQ, LANES, TQ), bf16),
            pltpu.VMEM((2, seq // TK, LANES, TK), bf16),
            pltpu.VMEM((seq // TQ, LANES, TQ), f32),
        ],
        compiler_params=pltpu.CompilerParams(
            dimension_semantics=("arbitrary", "arbitrary"), vmem_limit_bytes=VMEM_LIMIT),
    )(qkv3, qkv3, qkv3, go, tri)


def _sg_kernel(u_ref, v_ref, w_ref, b_ref, gv_ref, go_ref, o_ref):
    seq = u_ref.shape[1]
    h0 = lax.broadcasted_iota(jnp.int32, (1, LANES), 1) < HEAD_DIM
    causal = (lax.broadcasted_iota(jnp.int32, (CHUNK, CHUNK), 1)
              <= lax.broadcasted_iota(jnp.int32, (CHUNK, CHUNK), 0))
    w_cat = jnp.concatenate(
        [jnp.where(causal, w_ref[0], 0.0), jnp.where(causal, w_ref[1], 0.0)], axis=1).astype(bf16)
    bias = b_ref[...]
    gv = gv_ref[...]
    go = go_ref[...]

    chunks = [slice(c * CHUNK, (c + 1) * CHUNK) for c in range(seq // CHUNK)]
    rhs = []
    for rows in chunks:
        vn = _head_rms(v_ref[0, rows, :], gv, h0)
        rhs.append(jnp.concatenate(
            [jnp.where(h0, vn, 0.0), jnp.where(h0, 0.0, vn)], axis=0).astype(bf16))
    ys = [jnp.dot(w_cat, r, preferred_element_type=f32) + bias for r in rhs]
    for rows, y in zip(chunks, ys):
        o = _head_rms(u_ref[0, rows, :] * y, go, h0)
        o_ref[0, rows, :] = o.astype(o_ref.dtype)


def _spatial_gate(uv3, sg_w, bias_tiles, gv, go):
    b, seq, _ = uv3.shape
    blk = lambda off: pl.BlockSpec((1, seq, LANES), lambda i, p: (i, 0, off + p))
    vec = pl.BlockSpec((1, LANES), lambda i, p: (0, 0))
    return pl.pallas_call(
        _sg_kernel,
        name="spatial_gate",
        grid=(b, PAIRS),
        in_specs=[blk(0), blk(PAIRS),
                  pl.BlockSpec((2, CHUNK, CHUNK), lambda i, p: (p, 0, 0)),
                  pl.BlockSpec((CHUNK, LANES), lambda i, p: (0, p)),
                  vec, vec],
        out_specs=pl.BlockSpec((1, seq, LANES), lambda i, p: (i, 0, p)),
        out_shape=jax.ShapeDtypeStruct((b, seq, SG_WIDTH), bf16),
        compiler_params=pltpu.CompilerParams(
            dimension_semantics=("arbitrary", "arbitrary"), vmem_limit_bytes=VMEM_LIMIT),
    )(uv3, uv3, sg_w, bias_tiles, gv, go)


def _mlp_kernel(x_ref, sb_ref, sg_ref, wo_ref, g2_ref, w1_ref, w2_ref, o_ref, *, ff_chunk):
    mix = jnp.concatenate([sb_ref[...], sg_ref[...]], axis=1)
    h1 = x_ref[...] + jnp.dot(mix, wo_ref[...], preferred_element_type=f32)
    ms = jnp.mean(h1 * h1, axis=-1, keepdims=True)
    hn = (h1 * lax.rsqrt(ms + EPS) * g2_ref[...]).astype(bf16)
    ff = jnp.zeros_like(h1)
    for j in range(D_FF // ff_chunk):
        cols = slice(j * ff_chunk, (j + 1) * ff_chunk)
        a = jnp.maximum(jnp.dot(hn, w1_ref[:, cols], preferred_element_type=f32), 0.0)
        ff = ff + jnp.dot((a * a).astype(bf16), w2_ref[cols, :], preferred_element_type=f32)
    o_ref[...] = h1 + ff


def _mlp(x2, o_sb, o_sg, wo_bf, g2, w1_bf, w2_bf, tm, ff_chunk):
    m = x2.shape[0]
    const = lambda shape: pl.BlockSpec(shape, lambda i: (0, 0), pipeline_mode=pl.Buffered(1))
    return pl.pallas_call(
        functools.partial(_mlp_kernel, ff_chunk=ff_chunk),
        name="out_proj_mlp",
        grid=(m // tm,),
        in_specs=[
            pl.BlockSpec((tm, D_MODEL), lambda i: (i, 0)),
            pl.BlockSpec((tm, SB_WIDTH), lambda i: (i, 0)),
            pl.BlockSpec((tm, SG_WIDTH), lambda i: (i, 0)),
            const((D_MODEL, D_MODEL)),
            const((1, D_MODEL)),
            const((D_MODEL, D_FF)),
            const((D_FF, D_MODEL)),
        ],
        out_specs=pl.BlockSpec((tm, D_MODEL), lambda i: (i, 0)),
        out_shape=jax.ShapeDtypeStruct((m, D_MODEL), f32),
        compiler_params=pltpu.CompilerParams(
            dimension_semantics=("arbitrary",), vmem_limit_bytes=VMEM_LIMIT),
    )(x2, o_sb, o_sg, wo_bf, g2, w1_bf, w2_bf)


def kernel(x, norm1_g, w_in, q_norm_g, k_norm_g, sg_v_norm_g, sg_w, sg_b, sb_out_norm_g,
           sg_out_norm_g, w_out, norm2_g, w_ff1, w_ff2):
    b, seq, d = x.shape
    assert d == D_MODEL and seq % TQ == 0 and w_in.shape == (D_MODEL, IN_WIDTH)
    m = b * seq
    x2 = x.reshape(m, d)
    pair_gain = lambda g: jnp.tile(g.astype(f32), 2).reshape(1, LANES)

    qkv, uv = _in_proj(x2, norm1_g.reshape(1, d), w_in.astype(bf16), pair_gain(q_norm_g),
                       pair_gain(k_norm_g), tm=512, sub_rows=256)

    tri = (jnp.arange(TK)[None, :] >= jnp.arange(TK)[:, None]).astype(bf16)
    tri2 = jnp.concatenate([tri, tri], axis=1)
    o_sb = _sb_attn(qkv.reshape(b, seq, 3 * SB_WIDTH), pair_gain(sb_out_norm_g), tri2)

    bias_tiles = jnp.repeat(sg_b.astype(f32).T, HEAD_DIM, axis=1)
    o_sg = _spatial_gate(uv.reshape(b, seq, 2 * SG_WIDTH), sg_w, bias_tiles,
                         pair_gain(sg_v_norm_g), pair_gain(sg_out_norm_g))

    out = _mlp(x2, o_sb.reshape(m, SB_WIDTH), o_sg.reshape(m, SG_WIDTH), w_out.astype(bf16),
               norm2_g.reshape(1, d), w_ff1.astype(bf16), w_ff2.astype(bf16), tm=512, ff_chunk=1024)
    return out.reshape(b, seq, d)
```

```python
import functools
import math

import jax
import jax.numpy as jnp
from jax import lax
from jax.experimental import pallas as pl
from jax.experimental.pallas import tpu as pltpu

D_MODEL = 1024
HEAD_DIM = 64
SB_HEADS = 8
SG_GROUPS = 8
SB_WIDTH = SB_HEADS * HEAD_DIM
SG_WIDTH = SG_GROUPS * HEAD_DIM
IN_WIDTH = 3 * SB_WIDTH + 2 * SG_WIDTH
CHUNK = 128
D_FF = 4 * D_MODEL
EPS = 1e-6

LANES = 128
PAIRS = SB_WIDTH // LANES
TQ = 512
TK = 128
KT_PER_QB = TQ // TK
PIPE_DEPTH = 3
LOG2E = 1.4426950408889634
VMEM_LIMIT = 56 * 1024 * 1024

f32 = jnp.float32
bf16 = jnp.bfloat16


def _head_rms(t, g, h0):
    sq = t * t
    s0 = jnp.sum(jnp.where(h0, sq, 0.0), axis=-1, keepdims=True)
    s1 = jnp.sum(jnp.where(h0, 0.0, sq), axis=-1, keepdims=True)
    ms = jnp.where(h0, s0, s1) * (1.0 / HEAD_DIM)
    return t * lax.rsqrt(ms + EPS) * g


def _in_proj_kernel(x_ref, g_ref, w_ref, gq_ref, gk_ref, sgw_ref, sgb_ref, gv_ref, gs_ref,
                    qkv_ref, osg_ref, *, sub_rows):
    h0 = lax.broadcasted_iota(jnp.int32, (1, LANES), 1) < HEAD_DIM
    gq = gq_ref[...] * (LOG2E / math.sqrt(HEAD_DIM))
    gk = gk_ref[...]
    gv = gv_ref[...]
    gs = gs_ref[...]
    causal = (lax.broadcasted_iota(jnp.int32, (CHUNK, CHUNK), 1)
              <= lax.broadcasted_iota(jnp.int32, (CHUNK, CHUNK), 0))
    u_off = 3 * SB_WIDTH
    v_off = 3 * SB_WIDTH + SG_WIDTH
    w_cats = [jnp.concatenate([jnp.where(causal, sgw_ref[2 * t], 0.0),
                               jnp.where(causal, sgw_ref[2 * t + 1], 0.0)], axis=1).astype(bf16)
              for t in range(PAIRS)]

    def project(r):
        rows = slice(r * sub_rows, (r + 1) * sub_rows)
        x = x_ref[rows, :]
        ms = jnp.mean(x * x, axis=-1, keepdims=True)
        xn = (x * lax.rsqrt(ms + EPS) * g_ref[...]).astype(bf16)
        proj = jnp.dot(xn, w_ref[...], preferred_element_type=f32)
        for t in range(PAIRS):
            q_cols = slice(t * LANES, (t + 1) * LANES)
            k_cols = slice(SB_WIDTH + t * LANES, SB_WIDTH + (t + 1) * LANES)
            qkv_ref[rows, q_cols] = _head_rms(proj[:, q_cols], gq, h0).astype(bf16)
            qkv_ref[rows, k_cols] = _head_rms(proj[:, k_cols], gk, h0).astype(bf16)
        qkv_ref[rows, 2 * SB_WIDTH:] = proj[:, 2 * SB_WIDTH:3 * SB_WIDTH].astype(bf16)
        gate = []
        for c in range(sub_rows // CHUNK):
            crow = slice(c * CHUNK, (c + 1) * CHUNK)
            for t in range(PAIRS):
                vn = _head_rms(proj[crow, v_off + t * LANES:v_off + (t + 1) * LANES], gv, h0)
                rhs = jnp.concatenate(
                    [jnp.where(h0, vn, 0.0), jnp.where(h0, 0.0, vn)], axis=0).astype(bf16)
                gate.append((r * sub_rows + c * CHUNK, t, rhs,
                             proj[crow, u_off + t * LANES:u_off + (t + 1) * LANES]))
        return gate

    def spatial_gate(gate):
        for row0, t, rhs, u in gate:
            y = (jnp.dot(w_cats[t], rhs, preferred_element_type=f32)
                 + sgb_ref[:, t * LANES:(t + 1) * LANES])
            osg_ref[row0:row0 + CHUNK, t * LANES:(t + 1) * LANES] = (
                _head_rms(u * y, gs, h0).astype(bf16))

    pending = None
    for r in range(x_ref.shape[0] // sub_rows):
        gate = project(r)
        if pending is not None:
            spatial_gate(pending)
        pending = gate
    spatial_gate(pending)


def _in_proj(x2, g, w_bf, gq, gk, sg_w, bias_tiles, gv, gs, tm, sub_rows):
    m = x2.shape[0]
    vec = pl.BlockSpec((1, LANES), lambda i: (0, 0))
    return pl.pallas_call(
        functools.partial(_in_proj_kernel, sub_rows=sub_rows),
        name="in_proj",
        grid=(m // tm,),
        in_specs=[
            pl.BlockSpec((tm, D_MODEL), lambda i: (i, 0)),
            pl.BlockSpec((1, D_MODEL), lambda i: (0, 0)),
            pl.BlockSpec((D_MODEL, IN_WIDTH), lambda i: (0, 0), pipeline_mode=pl.Buffered(1)),
            vec, vec,
            pl.BlockSpec((SG_GROUPS, CHUNK, CHUNK), lambda i: (0, 0, 0)),
            pl.BlockSpec((CHUNK, SG_WIDTH), lambda i: (0, 0)),
            vec, vec,
        ],
        out_specs=[pl.BlockSpec((tm, 3 * SB_WIDTH), lambda i: (i, 0)),
                   pl.BlockSpec((tm, SG_WIDTH), lambda i: (i, 0))],
        out_shape=[jax.ShapeDtypeStruct((m, 3 * SB_WIDTH), bf16),
                   jax.ShapeDtypeStruct((m, SG_WIDTH), bf16)],
        compiler_params=pltpu.CompilerParams(
            dimension_semantics=("arbitrary",), vmem_limit_bytes=VMEM_LIMIT),
    )(x2, g, w_bf, gq, gk, sg_w, bias_tiles, gv, gs)


def _sb_attn_kernel(q_ref, k_ref, v_ref, go_ref, tri_ref, o_ref, qT_s, vT_s, acc_s):
    seq = q_ref.shape[1]
    nqb = seq // TQ
    sub0 = lax.broadcasted_iota(jnp.int32, (LANES, 1), 0) < HEAD_DIM
    go = go_ref[...]

    for i in range(nqb):
        rows = slice(i * TQ, (i + 1) * TQ)
        qT = q_ref[0, rows, :].astype(f32).T
        qT_s[0, i] = jnp.where(sub0, qT, 0.0).astype(bf16)
        qT_s[1, i] = jnp.where(sub0, 0.0, qT).astype(bf16)
        vT = v_ref[0, rows, :].astype(f32).T
        for j in range(TQ // TK):
            vt = vT[:, j * TK:(j + 1) * TK]
            vT_s[0, i * (TQ // TK) + j] = jnp.where(sub0, vt, 0.0).astype(bf16)
            vT_s[1, i * (TQ // TK) + j] = jnp.where(sub0, 0.0, vt).astype(bf16)

    def scores(k_t, qT, mask):
        z2 = jnp.dot(k_t, qT, preferred_element_type=f32)
        w = jnp.exp2(-jnp.abs(z2))
        sp = jnp.maximum(z2, 0.0) + jnp.log(1.0 + w) * LOG2E
        if mask is not None:
            sp = jnp.where(mask, sp, 0.0)
        hi = sp.astype(bf16)
        lo = (sp - hi.astype(f32)).astype(bf16)
        return z2, jnp.concatenate([hi, lo], axis=0)

    def weights(z2, hi_lo, c, mask):
        csum = jnp.dot(tri_ref[...], hi_lo, preferred_element_type=f32) + c
        a = jnp.exp2(z2 - csum)
        if mask is not None:
            a = jnp.where(mask, a, 0.0)
        return a.astype(bf16), csum[0:1, :]

    def key_rows(kt):
        return k_ref[0, kt * TK:(kt + 1) * TK, :]

    def finish_block(qb):
        acc = acc_s[qb]
        sq = acc * acc
        r0 = lax.rsqrt(jnp.sum(sq[:HEAD_DIM], axis=0, keepdims=True) * (1.0 / HEAD_DIM) + EPS)
        r1 = lax.rsqrt(jnp.sum(sq[HEAD_DIM:], axis=0, keepdims=True) * (1.0 / HEAD_DIM) + EPS)
        out = (acc * jnp.where(sub0, r0, r1)).T * go
        o_ref[0, qb * TQ:(qb + 1) * TQ, :] = out.astype(o_ref.dtype)

    tiles = []
    for qb in range(nqb):
        for j in reversed(range(KT_PER_QB)):
            width = TQ - j * TK
            mask = (lax.broadcasted_iota(jnp.int32, (TK, width), 0)
                    < lax.broadcasted_iota(jnp.int32, (TK, width), 1))
            tiles.append((qb, qb * KT_PER_QB + j, j * TK, mask))
        tiles += [(qb, kt, 0, None) for kt in reversed(range(qb * KT_PER_QB))]
    last_tile = {qb: max(i for i, t in enumerate(tiles) if t[0] == qb) for qb in range(nqb)}

    carry = {}
    probs = {}

    def finish(i, h, z2, hi_lo):
        qb, kt, lane, mask = tiles[i]
        c = carry.get((qb, h), jnp.zeros((1, TQ), f32))
        a, new = weights(z2, hi_lo, c[:, lane:], mask)
        carry[(qb, h)] = jnp.concatenate([c[:, :lane], new], axis=1) if lane else new
        probs[h] = a
        if h == 1:
            v_cat = jnp.concatenate([vT_s[0, kt], vT_s[1, kt]], axis=1)
            acc_s[qb, :, lane:] += jnp.dot(v_cat, jnp.concatenate([probs[0], probs[1]], axis=0),
                                           preferred_element_type=f32)
            if i == last_tile[qb]:
                finish_block(qb)

    for qb in range(nqb):
        acc_s[qb] = jnp.zeros((LANES, TQ), f32)
    in_flight = []
    for i, (qb, kt, lane, mask) in enumerate(tiles):
        for h in (0, 1):
            in_flight.append((i, h) + scores(key_rows(kt), qT_s[h, qb, :, lane:], mask))
            if len(in_flight) > PIPE_DEPTH:
                finish(*in_flight.pop(0))
    while in_flight:
        finish(*in_flight.pop(0))


def _sb_attn(qkv3, go, tri):
    b, seq, _ = qkv3.shape
    blk = lambda off: pl.BlockSpec((1, seq, LANES), lambda i, p: (i, 0, off + p))
    return pl.pallas_call(
        _sb_attn_kernel,
        name="sb_attn",
        grid=(b, PAIRS),
        in_specs=[blk(0), blk(PAIRS), blk(2 * PAIRS),
                  pl.BlockSpec((1, LANES), lambda i, p: (0, 0)),
                  pl.BlockSpec((TK, 2 * TK), lambda i, p: (0, 0))],
        out_specs=pl.BlockSpec((1, seq, LANES), lambda i, p: (i, 0, p)),
        out_shape=jax.ShapeDtypeStruct((b, seq, SB_WIDTH), bf16),
        scratch_shapes=[
            pltpu.VMEM((2, seq // TQ, LANES, TQ), bf16),
            pltpu.VMEM((2, seq // TK, LANES, TK), bf16),
            pltpu.VMEM((seq // TQ, LANES, TQ), f32),
        ],
        compiler_params=pltpu.CompilerParams(
            dimension_semantics=("arbitrary", "arbitrary"), vmem_limit_bytes=VMEM_LIMIT),
    )(qkv3, qkv3, qkv3, go, tri)


def _mlp_kernel(x_ref, sb_ref, sg_ref, wo_ref, g2_ref, w1_ref, w2_ref, o_ref, *, ff_chunk):
    mix = jnp.concatenate([sb_ref[...], sg_ref[...]], axis=1)
    h1 = x_ref[...] + jnp.dot(mix, wo_ref[...], preferred_element_type=f32)
    ms = jnp.mean(h1 * h1, axis=-1, keepdims=True)
    hn = (h1 * lax.rsqrt(ms + EPS) * g2_ref[...]).astype(bf16)
    ff = jnp.zeros_like(h1)
    for j in range(D_FF // ff_chunk):
        cols = slice(j * ff_chunk, (j + 1) * ff_chunk)
        a = jnp.maximum(jnp.dot(hn, w1_ref[:, cols], preferred_element_type=f32), 0.0)
        ff = ff + jnp.dot((a * a).astype(bf16), w2_ref[cols, :], preferred_element_type=f32)
    o_ref[...] = h1 + ff


def _mlp(x2, o_sb, o_sg, wo_bf, g2, w1_bf, w2_bf, tm, ff_chunk):
    m = x2.shape[0]
    const = lambda shape: pl.BlockSpec(shape, lambda i: (0, 0), pipeline_mode=pl.Buffered(1))
    return pl.pallas_call(
        functools.partial(_mlp_kernel, ff_chunk=ff_chunk),
        name="out_proj_mlp",
        grid=(m // tm,),
        in_specs=[
            pl.BlockSpec((tm, D_MODEL), lambda i: (i, 0)),
            pl.BlockSpec((tm, SB_WIDTH), lambda i: (i, 0)),
            pl.BlockSpec((tm, SG_WIDTH), lambda i: (i, 0)),
            const((D_MODEL, D_MODEL)),
            const((1, D_MODEL)),
            const((D_MODEL, D_FF)),
            const((D_FF, D_MODEL)),
        ],
        out_specs=pl.BlockSpec((tm, D_MODEL), lambda i: (i, 0)),
        out_shape=jax.ShapeDtypeStruct((m, D_MODEL), f32),
        compiler_params=pltpu.CompilerParams(
            dimension_semantics=("arbitrary",), vmem_limit_bytes=VMEM_LIMIT),
    )(x2, o_sb, o_sg, wo_bf, g2, w1_bf, w2_bf)


def kernel(x, norm1_g, w_in, q_norm_g, k_norm_g, sg_v_norm_g, sg_w, sg_b, sb_out_norm_g,
           sg_out_norm_g, w_out, norm2_g, w_ff1, w_ff2):
    b, seq, d = x.shape
    assert d == D_MODEL and seq % TQ == 0 and w_in.shape == (D_MODEL, IN_WIDTH)
    m = b * seq
    x2 = x.reshape(m, d)
    pair_gain = lambda g: jnp.tile(g.astype(f32), 2).reshape(1, LANES)

    bias_tiles = jnp.repeat(sg_b.astype(f32).T, HEAD_DIM, axis=1)
    qkv, o_sg = _in_proj(x2, norm1_g.reshape(1, d), w_in.astype(bf16), pair_gain(q_norm_g),
                         pair_gain(k_norm_g), sg_w, bias_tiles, pair_gain(sg_v_norm_g),
                         pair_gain(sg_out_norm_g), tm=1024, sub_rows=256)

    tri = (jnp.arange(TK)[None, :] >= jnp.arange(TK)[:, None]).astype(bf16)
    tri2 = jnp.concatenate([tri, tri], axis=1)
    o_sb = _sb_attn(qkv.reshape(b, seq, 3 * SB_WIDTH), pair_gain(sb_out_norm_g), tri2)

    out = _mlp(x2, o_sb.reshape(m, SB_WIDTH), o_sg, w_out.astype(bf16),
               norm2_g.reshape(1, d), w_ff1.astype(bf16), w_ff2.astype(bf16), tm=512, ff_chunk=1024)
    return out.reshape(b, seq, d)
```

```python
import functools
import math

import jax
import jax.numpy as jnp
from jax import lax
from jax.experimental import pallas as pl
from jax.experimental.pallas import tpu as pltpu

D_MODEL = 1024
HEAD_DIM = 64
SB_HEADS = 8
SG_GROUPS = 8
SB_WIDTH = SB_HEADS * HEAD_DIM
SG_WIDTH = SG_GROUPS * HEAD_DIM
IN_WIDTH = 3 * SB_WIDTH + 2 * SG_WIDTH
CHUNK = 128
D_FF = 4 * D_MODEL
EPS = 1e-6

LANES = 128
PAIRS = SB_WIDTH // LANES
TQ = 512
TK = 128
KT_PER_QB = TQ // TK
LAG_AB = 2
LAG_BC = 1
LOG2E = 1.4426950408889634
VMEM_LIMIT = 56 * 1024 * 1024

f32 = jnp.float32
bf16 = jnp.bfloat16


def _head_rms(t, g, h0):
    sq = t * t
    s0 = jnp.sum(jnp.where(h0, sq, 0.0), axis=-1, keepdims=True)
    s1 = jnp.sum(jnp.where(h0, 0.0, sq), axis=-1, keepdims=True)
    ms = jnp.where(h0, s0, s1) * (1.0 / HEAD_DIM)
    return t * lax.rsqrt(ms + EPS) * g


def _in_proj_kernel(x_ref, g_ref, w_ref, gq_ref, gk_ref, sgw_ref, sgb_ref, gv_ref, gs_ref,
                    qkv_ref, osg_ref, *, sub_rows):
    h0 = lax.broadcasted_iota(jnp.int32, (1, LANES), 1) < HEAD_DIM
    gq = gq_ref[...] * (LOG2E / math.sqrt(HEAD_DIM))
    gk = gk_ref[...]
    gv = gv_ref[...]
    gs = gs_ref[...]
    causal = (lax.broadcasted_iota(jnp.int32, (CHUNK, CHUNK), 1)
              <= lax.broadcasted_iota(jnp.int32, (CHUNK, CHUNK), 0))
    u_off = 3 * SB_WIDTH
    v_off = 3 * SB_WIDTH + SG_WIDTH
    w_cats = [jnp.concatenate([jnp.where(causal, sgw_ref[2 * t], 0.0),
                               jnp.where(causal, sgw_ref[2 * t + 1], 0.0)], axis=1).astype(bf16)
              for t in range(PAIRS)]

    def project(r):
        rows = slice(r * sub_rows, (r + 1) * sub_rows)
        x = x_ref[rows, :]
        ms = jnp.mean(x * x, axis=-1, keepdims=True)
        xn = (x * lax.rsqrt(ms + EPS) * g_ref[...]).astype(bf16)
        proj = jnp.dot(xn, w_ref[...], preferred_element_type=f32)
        for t in range(PAIRS):
            q_cols = slice(t * LANES, (t + 1) * LANES)
            k_cols = slice(SB_WIDTH + t * LANES, SB_WIDTH + (t + 1) * LANES)
            qkv_ref[rows, q_cols] = _head_rms(proj[:, q_cols], gq, h0).astype(bf16)
            qkv_ref[rows, k_cols] = _head_rms(proj[:, k_cols], gk, h0).astype(bf16)
        qkv_ref[rows, 2 * SB_WIDTH:] = proj[:, 2 * SB_WIDTH:3 * SB_WIDTH].astype(bf16)
        gate = []
        for c in range(sub_rows // CHUNK):
            crow = slice(c * CHUNK, (c + 1) * CHUNK)
            for t in range(PAIRS):
                vn = _head_rms(proj[crow, v_off + t * LANES:v_off + (t + 1) * LANES], gv, h0)
                rhs = jnp.concatenate(
                    [jnp.where(h0, vn, 0.0), jnp.where(h0, 0.0, vn)], axis=0).astype(bf16)
                gate.append((r * sub_rows + c * CHUNK, t, rhs,
                             proj[crow, u_off + t * LANES:u_off + (t + 1) * LANES]))
        return gate

    def spatial_gate(gate):
        for row0, t, rhs, u in gate:
            y = (jnp.dot(w_cats[t], rhs, preferred_element_type=f32)
                 + sgb_ref[:, t * LANES:(t + 1) * LANES])
            osg_ref[row0:row0 + CHUNK, t * LANES:(t + 1) * LANES] = (
                _head_rms(u * y, gs, h0).astype(bf16))

    pending = None
    for r in range(x_ref.shape[0] // sub_rows):
        gate = project(r)
        if pending is not None:
            spatial_gate(pending)
        pending = gate
    spatial_gate(pending)


def _in_proj(x2, g, w_bf, gq, gk, sg_w, bias_tiles, gv, gs, tm, sub_rows):
    m = x2.shape[0]
    vec = pl.BlockSpec((1, LANES), lambda i: (0, 0))
    return pl.pallas_call(
        functools.partial(_in_proj_kernel, sub_rows=sub_rows),
        name="in_proj",
        grid=(m // tm,),
        in_specs=[
            pl.BlockSpec((tm, D_MODEL), lambda i: (i, 0)),
            pl.BlockSpec((1, D_MODEL), lambda i: (0, 0)),
            pl.BlockSpec((D_MODEL, IN_WIDTH), lambda i: (0, 0), pipeline_mode=pl.Buffered(1)),
            vec, vec,
            pl.BlockSpec((SG_GROUPS, CHUNK, CHUNK), lambda i: (0, 0, 0)),
            pl.BlockSpec((CHUNK, SG_WIDTH), lambda i: (0, 0)),
            vec, vec,
        ],
        out_specs=[pl.BlockSpec((tm, 3 * SB_WIDTH), lambda i: (i, 0)),
                   pl.BlockSpec((tm, SG_WIDTH), lambda i: (i, 0))],
        out_shape=[jax.ShapeDtypeStruct((m, 3 * SB_WIDTH), bf16),
                   jax.ShapeDtypeStruct((m, SG_WIDTH), bf16)],
        compiler_params=pltpu.CompilerParams(
            dimension_semantics=("arbitrary",), vmem_limit_bytes=VMEM_LIMIT),
    )(x2, g, w_bf, gq, gk, sg_w, bias_tiles, gv, gs)


def _sb_attn_kernel(q_ref, k_ref, v_ref, go_ref, ntri_ref, tail_ref, o_ref,
                    qT_s, k2_s, lhs_s, vT_s, acc_s):
    seq = q_ref.shape[1]
    nqb = seq // TQ
    half_k = TK // 2
    h0 = lax.broadcasted_iota(jnp.int32, (1, LANES), 1) < HEAD_DIM
    sub0 = lax.broadcasted_iota(jnp.int32, (LANES, 1), 0) < HEAD_DIM
    go = go_ref[...]

    for i in range(nqb):
        rows = slice(i * TQ, (i + 1) * TQ)
        qT_s[i] = q_ref[0, rows, :].astype(f32).T.astype(bf16)
        vT = v_ref[0, rows, :].astype(f32).T
        for j in range(TQ // TK):
            vt = vT[:, j * TK:(j + 1) * TK]
            vT_s[0, i * (TQ // TK) + j] = jnp.where(sub0, vt, 0.0).astype(bf16)
            vT_s[1, i * (TQ // TK) + j] = jnp.where(sub0, 0.0, vt).astype(bf16)
    ntri = ntri_ref[...].astype(f32)
    for kt in range(seq // TK):
        k_t = k_ref[0, kt * TK:(kt + 1) * TK, :].astype(f32)
        k2_s[kt] = jnp.concatenate(
            [jnp.where(h0, k_t, 0.0), jnp.where(h0, 0.0, k_t)], axis=0).astype(bf16)
        lhs_s[kt, 0] = jnp.concatenate(
            [jnp.where(h0, k_t, ntri).astype(bf16), tail_ref[0]], axis=1)
        lhs_s[kt, 1] = jnp.concatenate(
            [jnp.where(h0, ntri, k_t).astype(bf16), tail_ref[1]], axis=1)

    def scores(kt, qT, mask):
        z2 = jnp.dot(k2_s[kt], qT, preferred_element_type=f32)
        w = jnp.exp2(-jnp.abs(z2))
        sp = jnp.maximum(z2, 0.0) + jnp.log(1.0 + w) * LOG2E
        if mask is not None:
            sp = jnp.where(mask, sp, 0.0)
        return sp.astype(bf16), (z2[0:1, :], z2[TK:TK + 1, :])

    def carry_rows(c):
        c1 = c.astype(bf16).astype(f32)
        c2 = (c - c1).astype(bf16).astype(f32)
        c3 = c - c1 - c2
        row = lax.broadcasted_iota(jnp.int32, (half_k, 1), 0)
        return jnp.where(row == 0, c1, jnp.where(row == 1, c2, jnp.where(row == 2, c3, 0.0))
                         ).astype(bf16)

    def weights(kt, h, qT, sp, z2_row0, c, mask):
        own = slice(h * half_k, (h + 1) * half_k)
        other = slice((1 - h) * half_k, (2 - h) * half_k)
        blocks = [qT[own], sp[other]] if h == 0 else [sp[other], qT[own]]
        rhs = jnp.concatenate(blocks + [sp[own], carry_rows(c)], axis=0)
        expo = jnp.dot(lhs_s[kt, h], rhs, preferred_element_type=f32)
        a = jnp.exp2(expo)
        if mask is not None:
            a = jnp.where(mask, a, 0.0)
        return a.astype(bf16), z2_row0 - expo[0:1, :]

    def finish_block(qb):
        acc = acc_s[qb]
        sq = acc * acc
        r0 = lax.rsqrt(jnp.sum(sq[:HEAD_DIM], axis=0, keepdims=True) * (1.0 / HEAD_DIM) + EPS)
        r1 = lax.rsqrt(jnp.sum(sq[HEAD_DIM:], axis=0, keepdims=True) * (1.0 / HEAD_DIM) + EPS)
        out = (acc * jnp.where(sub0, r0, r1)).T * go
        o_ref[0, qb * TQ:(qb + 1) * TQ, :] = out.astype(o_ref.dtype)

    tiles = []
    for qb in range(nqb):
        for j in reversed(range(KT_PER_QB)):
            width = TQ - j * TK
            mask = (lax.broadcasted_iota(jnp.int32, (TK, width), 0)
                    < lax.broadcasted_iota(jnp.int32, (TK, width), 1))
            tiles.append((qb, qb * KT_PER_QB + j, j * TK, mask))
        tiles += [(qb, kt, 0, None) for kt in reversed(range(qb * KT_PER_QB))]
    last_tile = {qb: max(i for i, t in enumerate(tiles) if t[0] == qb) for qb in range(nqb)}

    carry = {}

    def stage_b(i, sp, z2_rows):
        qb, kt, lane, mask = tiles[i]
        qT = qT_s[qb, :, lane:]
        probs = []
        for h in (0, 1):
            c = carry.get((qb, h), jnp.zeros((1, TQ), f32))
            a, new = weights(kt, h, qT, sp[h * TK:(h + 1) * TK], z2_rows[h], c[:, lane:], mask)
            carry[(qb, h)] = jnp.concatenate([c[:, :lane], new], axis=1) if lane else new
            probs.append(a)
        return i, jnp.concatenate(probs, axis=0)

    def stage_c(i, probs):
        qb, kt, lane, mask = tiles[i]
        v_cat = jnp.concatenate([vT_s[0, kt], vT_s[1, kt]], axis=1)
        acc_s[qb, :, lane:] += jnp.dot(v_cat, probs, preferred_element_type=f32)
        if i == last_tile[qb]:
            finish_block(qb)

    for qb in range(nqb):
        acc_s[qb] = jnp.zeros((LANES, TQ), f32)
    queue_b, queue_c = [], []
    for i, (qb, kt, lane, mask) in enumerate(tiles):
        both = None if mask is None else jnp.concatenate([mask, mask], axis=0)
        queue_b.append((i,) + scores(kt, qT_s[qb, :, lane:], both))
        if len(queue_b) > LAG_AB:
            queue_c.append(stage_b(*queue_b.pop(0)))
        if len(queue_c) > LAG_BC:
            stage_c(*queue_c.pop(0))
    while queue_b:
        queue_c.append(stage_b(*queue_b.pop(0)))
        if len(queue_c) > LAG_BC:
            stage_c(*queue_c.pop(0))
    while queue_c:
        stage_c(*queue_c.pop(0))


def _sb_attn(qkv3, go, ntri, tail):
    b, seq, _ = qkv3.shape
    blk = lambda off: pl.BlockSpec((1, seq, LANES), lambda i, p: (i, 0, off + p))
    return pl.pallas_call(
        _sb_attn_kernel,
        name="sb_attn",
        grid=(b, PAIRS),
        in_specs=[blk(0), blk(PAIRS), blk(2 * PAIRS),
                  pl.BlockSpec((1, LANES), lambda i, p: (0, 0)),
                  pl.BlockSpec((TK, TK), lambda i, p: (0, 0)),
                  pl.BlockSpec((2, TK, TK), lambda i, p: (0, 0, 0))],
        out_specs=pl.BlockSpec((1, seq, LANES), lambda i, p: (i, 0, p)),
        out_shape=jax.ShapeDtypeStruct((b, seq, SB_WIDTH), bf16),
        scratch_shapes=[
            pltpu.VMEM((seq // TQ, LANES, TQ), bf16),
            pltpu.VMEM((seq // TK, 2 * TK, LANES), bf16),
            pltpu.VMEM((seq // TK, 2, TK, 2 * TK), bf16),
            pltpu.VMEM((2, seq // TK, LANES, TK), bf16),
            pltpu.VMEM((seq // TQ, LANES, TQ), f32),
        ],
        compiler_params=pltpu.CompilerParams(
            dimension_semantics=("arbitrary", "arbitrary"), vmem_limit_bytes=VMEM_LIMIT),
    )(qkv3, qkv3, qkv3, go, ntri, tail)


def _mlp_kernel(x_ref, sb_ref, sg_ref, wo_ref, g2_ref, w1_ref, w2_ref, o_ref, *, ff_chunk):
    mix = jnp.concatenate([sb_ref[...], sg_ref[...]], axis=1)
    h1 = x_ref[...] + jnp.dot(mix, wo_ref[...], preferred_element_type=f32)
    ms = jnp.mean(h1 * h1, axis=-1, keepdims=True)
    hn = (h1 * lax.rsqrt(ms + EPS) * g2_ref[...]).astype(bf16)
    ff = jnp.zeros_like(h1)
    for j in range(D_FF // ff_chunk):
        cols = slice(j * ff_chunk, (j + 1) * ff_chunk)
        a = jnp.maximum(jnp.dot(hn, w1_ref[:, cols], preferred_element_type=f32), 0.0)
        ff = ff + jnp.dot((a * a).astype(bf16), w2_ref[cols, :], preferred_element_type=f32)
    o_ref[...] = h1 + ff


def _mlp(x2, o_sb, o_sg, wo_bf, g2, w1_bf, w2_bf, tm, ff_chunk):
    m = x2.shape[0]
    const = lambda shape: pl.BlockSpec(shape, lambda i: (0, 0), pipeline_mode=pl.Buffered(1))
    return pl.pallas_call(
        functools.partial(_mlp_kernel, ff_chunk=ff_chunk),
        name="out_proj_mlp",
        grid=(m // tm,),
        in_specs=[
            pl.BlockSpec((tm, D_MODEL), lambda i: (i, 0)),
            pl.BlockSpec((tm, SB_WIDTH), lambda i: (i, 0)),
            pl.BlockSpec((tm, SG_WIDTH), lambda i: (i, 0)),
            const((D_MODEL, D_MODEL)),
            const((1, D_MODEL)),
            const((D_MODEL, D_FF)),
            const((D_FF, D_MODEL)),
        ],
        out_specs=pl.BlockSpec((tm, D_MODEL), lambda i: (i, 0)),
        out_shape=jax.ShapeDtypeStruct((m, D_MODEL), f32),
        compiler_params=pltpu.CompilerParams(
            dimension_semantics=("arbitrary",), vmem_limit_bytes=VMEM_LIMIT),
    )(x2, o_sb, o_sg, wo_bf, g2, w1_bf, w2_bf)


def kernel(x, norm1_g, w_in, q_norm_g, k_norm_g, sg_v_norm_g, sg_w, sg_b, sb_out_norm_g,
           sg_out_norm_g, w_out, norm2_g, w_ff1, w_ff2):
    b, seq, d = x.shape
    assert d == D_MODEL and seq % TQ == 0 and w_in.shape == (D_MODEL, IN_WIDTH)
    m = b * seq
    x2 = x.reshape(m, d)
    pair_gain = lambda g: jnp.tile(g.astype(f32), 2).reshape(1, LANES)

    bias_tiles = jnp.repeat(sg_b.astype(f32).T, HEAD_DIM, axis=1)
    qkv, o_sg = _in_proj(x2, norm1_g.reshape(1, d), w_in.astype(bf16), pair_gain(q_norm_g),
                         pair_gain(k_norm_g), sg_w, bias_tiles, pair_gain(sg_v_norm_g),
                         pair_gain(sg_out_norm_g), tm=1024, sub_rows=256)

    tri = (jnp.arange(TK)[None, :] >= jnp.arange(TK)[:, None]).astype(f32)
    carry_cols = jnp.broadcast_to((jnp.arange(TK // 2)[None, :] < 3).astype(f32), (TK, TK // 2))
    tail = jnp.stack([jnp.concatenate([tri[:, h * (TK // 2):(h + 1) * (TK // 2)], carry_cols], axis=1)
                      for h in (0, 1)])
    o_sb = _sb_attn(qkv.reshape(b, seq, 3 * SB_WIDTH), pair_gain(sb_out_norm_g),
                    (-tri).astype(bf16), (-tail).astype(bf16))

    out = _mlp(x2, o_sb.reshape(m, SB_WIDTH), o_sg, w_out.astype(bf16),
               norm2_g.reshape(1, d), w_ff1.astype(bf16), w_ff2.astype(bf16), tm=512, ff_chunk=1024)
    return out.reshape(b, seq, d)
```

```python
import functools
import math

import jax
import jax.numpy as jnp
from jax import lax
from jax.experimental import pallas as pl
from jax.experimental.pallas import tpu as pltpu

D_MODEL = 1024
HEAD_DIM = 64
SB_HEADS = 8
SG_GROUPS = 8
SB_WIDTH = SB_HEADS * HEAD_DIM
SG_WIDTH = SG_GROUPS * HEAD_DIM
IN_WIDTH = 3 * SB_WIDTH + 2 * SG_WIDTH
CHUNK = 128
D_FF = 4 * D_MODEL
EPS = 1e-6

LANES = 128
PAIRS = SB_WIDTH // LANES
TQ = 512
TK = 128
KT_PER_QB = TQ // TK
LAG_AB = 2
LAG_BC = 1
LOG2E = 1.4426950408889634
VMEM_LIMIT = 56 * 1024 * 1024

f32 = jnp.float32
bf16 = jnp.bfloat16


def _head_rms(t, g, h0):
    sq = t * t
    s0 = jnp.sum(jnp.where(h0, sq, 0.0), axis=-1, keepdims=True)
    s1 = jnp.sum(jnp.where(h0, 0.0, sq), axis=-1, keepdims=True)
    ms = jnp.where(h0, s0, s1) * (1.0 / HEAD_DIM)
    return t * lax.rsqrt(ms + EPS) * g


def _in_proj_kernel(x_ref, g_ref, w_ref, gq_ref, gk_ref, sgw_ref, sgb_ref, gv_ref, gs_ref,
                    qkv_ref, osg_ref, *, sub_rows):
    h0 = lax.broadcasted_iota(jnp.int32, (1, LANES), 1) < HEAD_DIM
    gq = gq_ref[...] * (LOG2E / math.sqrt(HEAD_DIM))
    gk = gk_ref[...]
    gv = gv_ref[...]
    gs = gs_ref[...]
    causal = (lax.broadcasted_iota(jnp.int32, (CHUNK, CHUNK), 1)
              <= lax.broadcasted_iota(jnp.int32, (CHUNK, CHUNK), 0))
    u_off = 3 * SB_WIDTH
    v_off = 3 * SB_WIDTH + SG_WIDTH
    w_cats = [jnp.concatenate([jnp.where(causal, sgw_ref[2 * t], 0.0),
                               jnp.where(causal, sgw_ref[2 * t + 1], 0.0)], axis=1).astype(bf16)
              for t in range(PAIRS)]

    def project(r):
        rows = slice(r * sub_rows, (r + 1) * sub_rows)
        x = x_ref[rows, :]
        ms = jnp.mean(x * x, axis=-1, keepdims=True)
        xn = (x * lax.rsqrt(ms + EPS) * g_ref[...]).astype(bf16)
        proj = jnp.dot(xn, w_ref[...], preferred_element_type=f32)
        for t in range(PAIRS):
            q_cols = slice(t * LANES, (t + 1) * LANES)
            k_cols = slice(SB_WIDTH + t * LANES, SB_WIDTH + (t + 1) * LANES)
            qkv_ref[rows, q_cols] = _head_rms(proj[:, q_cols], gq, h0).astype(bf16)
            qkv_ref[rows, k_cols] = _head_rms(proj[:, k_cols], gk, h0).astype(bf16)
        qkv_ref[rows, 2 * SB_WIDTH:] = proj[:, 2 * SB_WIDTH:3 * SB_WIDTH].astype(bf16)
        gate = []
        for c in range(sub_rows // CHUNK):
            crow = slice(c * CHUNK, (c + 1) * CHUNK)
            for t in range(PAIRS):
                vn = _head_rms(proj[crow, v_off + t * LANES:v_off + (t + 1) * LANES], gv, h0)
                rhs = jnp.concatenate(
                    [jnp.where(h0, vn, 0.0), jnp.where(h0, 0.0, vn)], axis=0).astype(bf16)
                gate.append((r * sub_rows + c * CHUNK, t, rhs,
                             proj[crow, u_off + t * LANES:u_off + (t + 1) * LANES]))
        return gate

    def spatial_gate(gate):
        for row0, t, rhs, u in gate:
            y = (jnp.dot(w_cats[t], rhs, preferred_element_type=f32)
                 + sgb_ref[:, t * LANES:(t + 1) * LANES])
            osg_ref[row0:row0 + CHUNK, t * LANES:(t + 1) * LANES] = (
                _head_rms(u * y, gs, h0).astype(bf16))

    pending = None
    for r in range(x_ref.shape[0] // sub_rows):
        gate = project(r)
        if pending is not None:
            spatial_gate(pending)
        pending = gate
    spatial_gate(pending)


def _in_proj(x2, g, w_bf, gq, gk, sg_w, bias_tiles, gv, gs, tm, sub_rows):
    m = x2.shape[0]
    vec = pl.BlockSpec((1, LANES), lambda i: (0, 0))
    return pl.pallas_call(
        functools.partial(_in_proj_kernel, sub_rows=sub_rows),
        name="in_proj",
        grid=(m // tm,),
        in_specs=[
            pl.BlockSpec((tm, D_MODEL), lambda i: (i, 0)),
            pl.BlockSpec((1, D_MODEL), lambda i: (0, 0)),
            pl.BlockSpec((D_MODEL, IN_WIDTH), lambda i: (0, 0), pipeline_mode=pl.Buffered(1)),
            vec, vec,
            pl.BlockSpec((SG_GROUPS, CHUNK, CHUNK), lambda i: (0, 0, 0)),
            pl.BlockSpec((CHUNK, SG_WIDTH), lambda i: (0, 0)),
            vec, vec,
        ],
        out_specs=[pl.BlockSpec((tm, 3 * SB_WIDTH), lambda i: (i, 0)),
                   pl.BlockSpec((tm, SG_WIDTH), lambda i: (i, 0))],
        out_shape=[jax.ShapeDtypeStruct((m, 3 * SB_WIDTH), bf16),
                   jax.ShapeDtypeStruct((m, SG_WIDTH), bf16)],
        compiler_params=pltpu.CompilerParams(
            dimension_semantics=("arbitrary",), vmem_limit_bytes=VMEM_LIMIT),
    )(x2, g, w_bf, gq, gk, sg_w, bias_tiles, gv, gs)


def _sb_attn_kernel(zero_ref, q_ref, k_ref, v_ref, go_ref, ntri_ref, tail_ref, o_ref,
                    qT_s, k2_s, lhs_s, vT_s, acc_s, sp_s, probs_s):
    seq = q_ref.shape[1]
    nqb = seq // TQ
    half_k = TK // 2
    h0 = lax.broadcasted_iota(jnp.int32, (1, LANES), 1) < HEAD_DIM
    sub0 = lax.broadcasted_iota(jnp.int32, (LANES, 1), 0) < HEAD_DIM
    go = go_ref[...]
    zero = zero_ref[0]

    for i in range(nqb):
        rows = slice(i * TQ, (i + 1) * TQ)
        qT_s[i] = q_ref[0, rows, :].astype(f32).T.astype(bf16)
        vT = v_ref[0, rows, :].astype(f32).T
        for j in range(TQ // TK):
            vt = vT[:, j * TK:(j + 1) * TK]
            vT_s[0, i * (TQ // TK) + j] = jnp.where(sub0, vt, 0.0).astype(bf16)
            vT_s[1, i * (TQ // TK) + j] = jnp.where(sub0, 0.0, vt).astype(bf16)
    ntri = ntri_ref[...].astype(f32)
    for kt in range(seq // TK):
        k_t = k_ref[0, kt * TK:(kt + 1) * TK, :].astype(f32)
        k2_s[kt] = jnp.concatenate(
            [jnp.where(h0, k_t, 0.0), jnp.where(h0, 0.0, k_t)], axis=0).astype(bf16)
        lhs_s[kt, 0] = jnp.concatenate(
            [jnp.where(h0, k_t, ntri).astype(bf16), tail_ref[0]], axis=1)
        lhs_s[kt, 1] = jnp.concatenate(
            [jnp.where(h0, ntri, k_t).astype(bf16), tail_ref[1]], axis=1)

    def scores(i, kt, qT, lane, mask):
        z2 = jnp.dot(k2_s[kt], qT, preferred_element_type=f32)
        w = jnp.exp2(-jnp.abs(z2))
        sp = jnp.maximum(z2, 0.0) + jnp.log(1.0 + w) * LOG2E
        if mask is not None:
            sp = jnp.where(mask, sp, 0.0)
        sp_s[zero + i % (LAG_AB + 1), :, lane:] = sp.astype(bf16)
        return (z2[0:1, :], z2[TK:TK + 1, :])

    def carry_rows(c):
        c1 = c.astype(bf16).astype(f32)
        c2 = (c - c1).astype(bf16).astype(f32)
        c3 = c - c1 - c2
        row = lax.broadcasted_iota(jnp.int32, (half_k, 1), 0)
        return jnp.where(row == 0, c1, jnp.where(row == 1, c2, jnp.where(row == 2, c3, 0.0))
                         ).astype(bf16)

    def weights(kt, h, qT, sp, z2_row0, c, mask):
        own = slice(h * half_k, (h + 1) * half_k)
        other = slice((1 - h) * half_k, (2 - h) * half_k)
        blocks = [qT(own), sp(other)] if h == 0 else [sp(other), qT(own)]
        rhs = jnp.concatenate(blocks + [sp(own), carry_rows(c)], axis=0)
        expo = jnp.dot(lhs_s[kt, h], rhs, preferred_element_type=f32)
        a = jnp.exp2(expo)
        if mask is not None:
            a = jnp.where(mask, a, 0.0)
        return a.astype(bf16), z2_row0 - expo[0:1, :]

    def finish_block(qb):
        acc = acc_s[qb]
        sq = acc * acc
        r0 = lax.rsqrt(jnp.sum(sq[:HEAD_DIM], axis=0, keepdims=True) * (1.0 / HEAD_DIM) + EPS)
        r1 = lax.rsqrt(jnp.sum(sq[HEAD_DIM:], axis=0, keepdims=True) * (1.0 / HEAD_DIM) + EPS)
        out = (acc * jnp.where(sub0, r0, r1)).T * go
        o_ref[0, qb * TQ:(qb + 1) * TQ, :] = out.astype(o_ref.dtype)

    tiles = []
    for qb in range(nqb):
        for j in reversed(range(KT_PER_QB)):
            width = TQ - j * TK
            mask = (lax.broadcasted_iota(jnp.int32, (TK, width), 0)
                    < lax.broadcasted_iota(jnp.int32, (TK, width), 1))
            tiles.append((qb, qb * KT_PER_QB + j, j * TK, mask))
        tiles += [(qb, kt, 0, None) for kt in reversed(range(qb * KT_PER_QB))]
    last_tile = {qb: max(i for i, t in enumerate(tiles) if t[0] == qb) for qb in range(nqb)}

    carry = {}

    def stage_b(i, z2_rows):
        qb, kt, lane, mask = tiles[i]
        slot = zero + i % (LAG_AB + 1)
        for h in (0, 1):
            c = carry.get((qb, h), jnp.zeros((1, TQ), f32))
            qT = lambda r: qT_s[qb, r, lane:]
            sp = lambda r, h=h: sp_s[slot, h * TK + r.start:h * TK + r.stop, lane:]
            a, new = weights(kt, h, qT, sp, z2_rows[h], c[:, lane:], mask)
            carry[(qb, h)] = jnp.concatenate([c[:, :lane], new], axis=1) if lane else new
            probs_s[zero + i % (LAG_BC + 1), h * TK:(h + 1) * TK, lane:] = a
        return (i,)

    def stage_c(i):
        qb, kt, lane, mask = tiles[i]
        v_cat = jnp.concatenate([vT_s[0, kt], vT_s[1, kt]], axis=1)
        acc_s[qb, :, lane:] += jnp.dot(v_cat, probs_s[zero + i % (LAG_BC + 1), :, lane:],
                                       preferred_element_type=f32)
        if i == last_tile[qb]:
            finish_block(qb)

    for qb in range(nqb):
        acc_s[qb] = jnp.zeros((LANES, TQ), f32)
    queue_b, queue_c = [], []
    for i, (qb, kt, lane, mask) in enumerate(tiles):
        both = None if mask is None else jnp.concatenate([mask, mask], axis=0)
        queue_b.append((i, scores(i, kt, qT_s[qb, :, lane:], lane, both)))
        if len(queue_b) > LAG_AB:
            queue_c.append(stage_b(*queue_b.pop(0)))
        if len(queue_c) > LAG_BC:
            stage_c(*queue_c.pop(0))
    while queue_b:
        queue_c.append(stage_b(*queue_b.pop(0)))
        if len(queue_c) > LAG_BC:
            stage_c(*queue_c.pop(0))
    while queue_c:
        stage_c(*queue_c.pop(0))


def _sb_attn(qkv3, go, ntri, tail):
    b, seq, _ = qkv3.shape
    blk = lambda off: pl.BlockSpec((1, seq, LANES), lambda i, p: (i, 0, off + p))
    return pl.pallas_call(
        _sb_attn_kernel,
        name="sb_attn",
        grid=(b, PAIRS),
        in_specs=[pl.BlockSpec(memory_space=pltpu.SMEM),
                  blk(0), blk(PAIRS), blk(2 * PAIRS),
                  pl.BlockSpec((1, LANES), lambda i, p: (0, 0)),
                  pl.BlockSpec((TK, TK), lambda i, p: (0, 0)),
                  pl.BlockSpec((2, TK, TK), lambda i, p: (0, 0, 0))],
        out_specs=pl.BlockSpec((1, seq, LANES), lambda i, p: (i, 0, p)),
        out_shape=jax.ShapeDtypeStruct((b, seq, SB_WIDTH), bf16),
        scratch_shapes=[
            pltpu.VMEM((seq // TQ, LANES, TQ), bf16),
            pltpu.VMEM((seq // TK, 2 * TK, LANES), bf16),
            pltpu.VMEM((seq // TK, 2, TK, 2 * TK), bf16),
            pltpu.VMEM((2, seq // TK, LANES, TK), bf16),
            pltpu.VMEM((seq // TQ, LANES, TQ), f32),
            pltpu.VMEM((LAG_AB + 1, 2 * TK, TQ), bf16),
            pltpu.VMEM((LAG_BC + 1, 2 * TK, TQ), bf16),
        ],
        compiler_params=pltpu.CompilerParams(
            dimension_semantics=("arbitrary", "arbitrary"), vmem_limit_bytes=VMEM_LIMIT),
    )(jnp.zeros((1,), jnp.int32), qkv3, qkv3, qkv3, go, ntri, tail)


def _mlp_kernel(x_ref, sb_ref, sg_ref, wo_ref, g2_ref, w1_ref, w2_ref, o_ref, *, ff_chunk):
    mix = jnp.concatenate([sb_ref[...], sg_ref[...]], axis=1)
    h1 = x_ref[...] + jnp.dot(mix, wo_ref[...], preferred_element_type=f32)
    ms = jnp.mean(h1 * h1, axis=-1, keepdims=True)
    hn = (h1 * lax.rsqrt(ms + EPS) * g2_ref[...]).astype(bf16)
    ff = jnp.zeros_like(h1)
    for j in range(D_FF // ff_chunk):
        cols = slice(j * ff_chunk, (j + 1) * ff_chunk)
        a = jnp.maximum(jnp.dot(hn, w1_ref[:, cols], preferred_element_type=f32), 0.0)
        ff = ff + jnp.dot((a * a).astype(bf16), w2_ref[cols, :], preferred_element_type=f32)
    o_ref[...] = h1 + ff


def _mlp(x2, o_sb, o_sg, wo_bf, g2, w1_bf, w2_bf, tm, ff_chunk):
    m = x2.shape[0]
    const = lambda shape: pl.BlockSpec(shape, lambda i: (0, 0), pipeline_mode=pl.Buffered(1))
    return pl.pallas_call(
        functools.partial(_mlp_kernel, ff_chunk=ff_chunk),
        name="out_proj_mlp",
        grid=(m // tm,),
        in_specs=[
            pl.BlockSpec((tm, D_MODEL), lambda i: (i, 0)),
            pl.BlockSpec((tm, SB_WIDTH), lambda i: (i, 0)),
            pl.BlockSpec((tm, SG_WIDTH), lambda i: (i, 0)),
            const((D_MODEL, D_MODEL)),
            const((1, D_MODEL)),
            const((D_MODEL, D_FF)),
            const((D_FF, D_MODEL)),
        ],
        out_specs=pl.BlockSpec((tm, D_MODEL), lambda i: (i, 0)),
        out_shape=jax.ShapeDtypeStruct((m, D_MODEL), f32),
        compiler_params=pltpu.CompilerParams(
            dimension_semantics=("arbitrary",), vmem_limit_bytes=VMEM_LIMIT),
    )(x2, o_sb, o_sg, wo_bf, g2, w1_bf, w2_bf)


def kernel(x, norm1_g, w_in, q_norm_g, k_norm_g, sg_v_norm_g, sg_w, sg_b, sb_out_norm_g,
           sg_out_norm_g, w_out, norm2_g, w_ff1, w_ff2):
    b, seq, d = x.shape
    assert d == D_MODEL and seq % TQ == 0 and w_in.shape == (D_MODEL, IN_WIDTH)
    m = b * seq
    x2 = x.reshape(m, d)
    pair_gain = lambda g: jnp.tile(g.astype(f32), 2).reshape(1, LANES)

    bias_tiles = jnp.repeat(sg_b.astype(f32).T, HEAD_DIM, axis=1)
    qkv, o_sg = _in_proj(x2, norm1_g.reshape(1, d), w_in.astype(bf16), pair_gain(q_norm_g),
                         pair_gain(k_norm_g), sg_w, bias_tiles, pair_gain(sg_v_norm_g),
                         pair_gain(sg_out_norm_g), tm=2048, sub_rows=256)

    tri = (jnp.arange(TK)[None, :] >= jnp.arange(TK)[:, None]).astype(f32)
    carry_cols = jnp.broadcast_to((jnp.arange(TK // 2)[None, :] < 3).astype(f32), (TK, TK // 2))
    tail = jnp.stack([jnp.concatenate([tri[:, h * (TK // 2):(h + 1) * (TK // 2)], carry_cols], axis=1)
                      for h in (0, 1)])
    o_sb = _sb_attn(qkv.reshape(b, seq, 3 * SB_WIDTH), pair_gain(sb_out_norm_g),
                    (-tri).astype(bf16), (-tail).astype(bf16))

    out = _mlp(x2, o_sb.reshape(m, SB_WIDTH), o_sg, w_out.astype(bf16),
               norm2_g.reshape(1, d), w_ff1.astype(bf16), w_ff2.astype(bf16), tm=512, ff_chunk=1024)
    return out.reshape(b, seq, d)
```

```python
import functools
import math

import jax
import jax.numpy as jnp
from jax import lax
from jax.experimental import pallas as pl
from jax.experimental.pallas import tpu as pltpu

D_MODEL = 1024
HEAD_DIM = 64
SB_HEADS = 8
SG_GROUPS = 8
SB_WIDTH = SB_HEADS * HEAD_DIM
SG_WIDTH = SG_GROUPS * HEAD_DIM
IN_WIDTH = 3 * SB_WIDTH + 2 * SG_WIDTH
CHUNK = 128
D_FF = 4 * D_MODEL
EPS = 1e-6

LANES = 128
PAIRS = SB_WIDTH // LANES
TQ = 512
TK = 128
KT_PER_QB = TQ // TK
LAG_AB = 2
LAG_BC = 1
NEAR_TILES = 2
UNDERFLOW_LOG2 = 160.0
LOG2E = 1.4426950408889634
VMEM_LIMIT = 56 * 1024 * 1024

f32 = jnp.float32
bf16 = jnp.bfloat16


def _head_rms(t, g, h0):
    sq = t * t
    s0 = jnp.sum(jnp.where(h0, sq, 0.0), axis=-1, keepdims=True)
    s1 = jnp.sum(jnp.where(h0, 0.0, sq), axis=-1, keepdims=True)
    ms = jnp.where(h0, s0, s1) * (1.0 / HEAD_DIM)
    return t * lax.rsqrt(ms + EPS) * g


def _in_proj_kernel(x_ref, g_ref, w_ref, gq_ref, gk_ref, sgw_ref, sgb_ref, gv_ref, gs_ref,
                    qkv_ref, osg_ref, *, sub_rows):
    h0 = lax.broadcasted_iota(jnp.int32, (1, LANES), 1) < HEAD_DIM
    gq = gq_ref[...] * (LOG2E / math.sqrt(HEAD_DIM))
    gk = gk_ref[...]
    gv = gv_ref[...]
    gs = gs_ref[...]
    causal = (lax.broadcasted_iota(jnp.int32, (CHUNK, CHUNK), 1)
              <= lax.broadcasted_iota(jnp.int32, (CHUNK, CHUNK), 0))
    u_off = 3 * SB_WIDTH
    v_off = 3 * SB_WIDTH + SG_WIDTH
    w_cats = [jnp.concatenate([jnp.where(causal, sgw_ref[2 * t], 0.0),
                               jnp.where(causal, sgw_ref[2 * t + 1], 0.0)], axis=1).astype(bf16)
              for t in range(PAIRS)]

    def project(r):
        rows = slice(r * sub_rows, (r + 1) * sub_rows)
        x = x_ref[rows, :]
        ms = jnp.mean(x * x, axis=-1, keepdims=True)
        xn = (x * lax.rsqrt(ms + EPS) * g_ref[...]).astype(bf16)
        proj = jnp.dot(xn, w_ref[...], preferred_element_type=f32)
        for t in range(PAIRS):
            q_cols = slice(t * LANES, (t + 1) * LANES)
            k_cols = slice(SB_WIDTH + t * LANES, SB_WIDTH + (t + 1) * LANES)
            qkv_ref[rows, q_cols] = _head_rms(proj[:, q_cols], gq, h0).astype(bf16)
            qkv_ref[rows, k_cols] = _head_rms(proj[:, k_cols], gk, h0).astype(bf16)
        qkv_ref[rows, 2 * SB_WIDTH:] = proj[:, 2 * SB_WIDTH:3 * SB_WIDTH].astype(bf16)
        gate = []
        for c in range(sub_rows // CHUNK):
            crow = slice(c * CHUNK, (c + 1) * CHUNK)
            for t in range(PAIRS):
                vn = _head_rms(proj[crow, v_off + t * LANES:v_off + (t + 1) * LANES], gv, h0)
                rhs = jnp.concatenate(
                    [jnp.where(h0, vn, 0.0), jnp.where(h0, 0.0, vn)], axis=0).astype(bf16)
                gate.append((r * sub_rows + c * CHUNK, t, rhs,
                             proj[crow, u_off + t * LANES:u_off + (t + 1) * LANES]))
        return gate

    def spatial_gate(gate):
        for row0, t, rhs, u in gate:
            y = (jnp.dot(w_cats[t], rhs, preferred_element_type=f32)
                 + sgb_ref[:, t * LANES:(t + 1) * LANES])
            osg_ref[row0:row0 + CHUNK, t * LANES:(t + 1) * LANES] = (
                _head_rms(u * y, gs, h0).astype(bf16))

    pending = None
    for r in range(x_ref.shape[0] // sub_rows):
        gate = project(r)
        if pending is not None:
            spatial_gate(pending)
        pending = gate
    spatial_gate(pending)


def _in_proj(x2, g, w_bf, gq, gk, sg_w, bias_tiles, gv, gs, tm, sub_rows):
    m = x2.shape[0]
    vec = pl.BlockSpec((1, LANES), lambda i: (0, 0))
    return pl.pallas_call(
        functools.partial(_in_proj_kernel, sub_rows=sub_rows),
        name="in_proj",
        grid=(m // tm,),
        in_specs=[
            pl.BlockSpec((tm, D_MODEL), lambda i: (i, 0)),
            pl.BlockSpec((1, D_MODEL), lambda i: (0, 0)),
            pl.BlockSpec((D_MODEL, IN_WIDTH), lambda i: (0, 0), pipeline_mode=pl.Buffered(1)),
            vec, vec,
            pl.BlockSpec((SG_GROUPS, CHUNK, CHUNK), lambda i: (0, 0, 0)),
            pl.BlockSpec((CHUNK, SG_WIDTH), lambda i: (0, 0)),
            vec, vec,
        ],
        out_specs=[pl.BlockSpec((tm, 3 * SB_WIDTH), lambda i: (i, 0)),
                   pl.BlockSpec((tm, SG_WIDTH), lambda i: (i, 0))],
        out_shape=[jax.ShapeDtypeStruct((m, 3 * SB_WIDTH), bf16),
                   jax.ShapeDtypeStruct((m, SG_WIDTH), bf16)],
        compiler_params=pltpu.CompilerParams(
            dimension_semantics=("arbitrary",), vmem_limit_bytes=VMEM_LIMIT),
    )(x2, g, w_bf, gq, gk, sg_w, bias_tiles, gv, gs)


def _sb_attn_kernel(q_ref, k_ref, v_ref, go_ref, ntri_ref, tail_ref, o_ref,
                    qT_s, k2_s, lhs_s, vT_s, acc_s):
    seq = q_ref.shape[1]
    nqb = seq // TQ
    half_k = TK // 2
    h0 = lax.broadcasted_iota(jnp.int32, (1, LANES), 1) < HEAD_DIM
    sub0 = lax.broadcasted_iota(jnp.int32, (LANES, 1), 0) < HEAD_DIM
    go = go_ref[...]

    for i in range(nqb):
        rows = slice(i * TQ, (i + 1) * TQ)
        qT_s[i] = q_ref[0, rows, :].astype(f32).T.astype(bf16)
        vT = v_ref[0, rows, :].astype(f32).T
        for j in range(TQ // TK):
            vt = vT[:, j * TK:(j + 1) * TK]
            vT_s[0, i * (TQ // TK) + j] = jnp.where(sub0, vt, 0.0).astype(bf16)
            vT_s[1, i * (TQ // TK) + j] = jnp.where(sub0, 0.0, vt).astype(bf16)
    ntri = ntri_ref[...].astype(f32)
    for kt in range(seq // TK):
        k_t = k_ref[0, kt * TK:(kt + 1) * TK, :].astype(f32)
        k2_s[kt] = jnp.concatenate(
            [jnp.where(h0, k_t, 0.0), jnp.where(h0, 0.0, k_t)], axis=0).astype(bf16)
        lhs_s[kt, 0] = jnp.concatenate(
            [jnp.where(h0, k_t, ntri).astype(bf16), tail_ref[0]], axis=1)
        lhs_s[kt, 1] = jnp.concatenate(
            [jnp.where(h0, ntri, k_t).astype(bf16), tail_ref[1]], axis=1)

    def scores(kt, qT, mask):
        z2 = jnp.dot(k2_s[kt], qT, preferred_element_type=f32)
        w = jnp.exp2(-jnp.abs(z2))
        sp = jnp.maximum(z2, 0.0) + jnp.log(1.0 + w) * LOG2E
        if mask is not None:
            sp = jnp.where(mask, sp, 0.0)
        return sp.astype(bf16), (z2[0:1, :], z2[TK:TK + 1, :])

    def carry_rows(c):
        c1 = c.astype(bf16).astype(f32)
        c2 = (c - c1).astype(bf16).astype(f32)
        c3 = c - c1 - c2
        row = lax.broadcasted_iota(jnp.int32, (half_k, 1), 0)
        return jnp.where(row == 0, c1, jnp.where(row == 1, c2, jnp.where(row == 2, c3, 0.0))
                         ).astype(bf16)

    def weights(kt, h, qT, sp, z2_row0, c, mask):
        own = slice(h * half_k, (h + 1) * half_k)
        other = slice((1 - h) * half_k, (2 - h) * half_k)
        blocks = [qT[own], sp[other]] if h == 0 else [sp[other], qT[own]]
        rhs = jnp.concatenate(blocks + [sp[own], carry_rows(c)], axis=0)
        expo = jnp.dot(lhs_s[kt, h], rhs, preferred_element_type=f32)
        a = jnp.exp2(expo)
        if mask is not None:
            a = jnp.where(mask, a, 0.0)
        return a.astype(bf16), z2_row0 - expo[0:1, :]

    def finish_block(qb):
        acc = acc_s[qb]
        sq = acc * acc
        r0 = lax.rsqrt(jnp.sum(sq[:HEAD_DIM], axis=0, keepdims=True) * (1.0 / HEAD_DIM) + EPS)
        r1 = lax.rsqrt(jnp.sum(sq[HEAD_DIM:], axis=0, keepdims=True) * (1.0 / HEAD_DIM) + EPS)
        out = (acc * jnp.where(sub0, r0, r1)).T * go
        o_ref[0, qb * TQ:(qb + 1) * TQ, :] = out.astype(o_ref.dtype)

    def run_tiles(tiles, carry):
        def stage_b(i, sp, z2_rows):
            qb, kt, lane, mask = tiles[i]
            qT = qT_s[qb, :, lane:]
            probs = []
            for h in (0, 1):
                c = carry[(qb, h)]
                a, new = weights(kt, h, qT, sp[h * TK:(h + 1) * TK], z2_rows[h], c[:, lane:], mask)
                carry[(qb, h)] = jnp.concatenate([c[:, :lane], new], axis=1) if lane else new
                probs.append(a)
            return i, jnp.concatenate(probs, axis=0)

        def stage_c(i, probs):
            qb, kt, lane, mask = tiles[i]
            v_cat = jnp.concatenate([vT_s[0, kt], vT_s[1, kt]], axis=1)
            acc_s[qb, :, lane:] += jnp.dot(v_cat, probs, preferred_element_type=f32)

        queue_b, queue_c = [], []
        for i, (qb, kt, lane, mask) in enumerate(tiles):
            both = None if mask is None else jnp.concatenate([mask, mask], axis=0)
            queue_b.append((i,) + scores(kt, qT_s[qb, :, lane:], both))
            if len(queue_b) > LAG_AB:
                queue_c.append(stage_b(*queue_b.pop(0)))
            if len(queue_c) > LAG_BC:
                stage_c(*queue_c.pop(0))
        while queue_b:
            queue_c.append(stage_b(*queue_b.pop(0)))
            if len(queue_c) > LAG_BC:
                stage_c(*queue_c.pop(0))
        while queue_c:
            stage_c(*queue_c.pop(0))

    near = []
    for qb in range(nqb):
        for j in reversed(range(KT_PER_QB)):
            width = TQ - j * TK
            mask = (lax.broadcasted_iota(jnp.int32, (TK, width), 0)
                    < lax.broadcasted_iota(jnp.int32, (TK, width), 1))
            near.append((qb, qb * KT_PER_QB + j, j * TK, mask))
        first_far = max(qb * KT_PER_QB - NEAR_TILES, 0)
        near += [(qb, kt, 0, None) for kt in reversed(range(first_far, qb * KT_PER_QB))]
    for qb in range(nqb):
        acc_s[qb] = jnp.zeros((LANES, TQ), f32)
    carry = {(qb, h): jnp.zeros((1, TQ), f32) for qb in range(nqb) for h in (0, 1)}
    run_tiles(near, carry)

    for qb in range(nqb):
        first_far = max(qb * KT_PER_QB - NEAR_TILES, 0)
        if first_far:
            c0, c1 = carry[(qb, 0)], carry[(qb, 1)]

            @pl.when(jnp.min(jnp.minimum(c0, c1)) < UNDERFLOW_LOG2)
            def _(qb=qb, first_far=first_far, c0=c0, c1=c1):
                run_tiles([(qb, kt, 0, None) for kt in reversed(range(first_far))],
                          {(qb, 0): c0, (qb, 1): c1})

    for qb in range(nqb):
        finish_block(qb)


def _sb_attn(qkv3, go, ntri, tail):
    b, seq, _ = qkv3.shape
    blk = lambda off: pl.BlockSpec((1, seq, LANES), lambda i, p: (i, 0, off + p))
    return pl.pallas_call(
        _sb_attn_kernel,
        name="sb_attn",
        grid=(b, PAIRS),
        in_specs=[blk(0), blk(PAIRS), blk(2 * PAIRS),
                  pl.BlockSpec((1, LANES), lambda i, p: (0, 0)),
                  pl.BlockSpec((TK, TK), lambda i, p: (0, 0)),
                  pl.BlockSpec((2, TK, TK), lambda i, p: (0, 0, 0))],
        out_specs=pl.BlockSpec((1, seq, LANES), lambda i, p: (i, 0, p)),
        out_shape=jax.ShapeDtypeStruct((b, seq, SB_WIDTH), bf16),
        scratch_shapes=[
            pltpu.VMEM((seq // TQ, LANES, TQ), bf16),
            pltpu.VMEM((seq // TK, 2 * TK, LANES), bf16),
            pltpu.VMEM((seq // TK, 2, TK, 2 * TK), bf16),
            pltpu.VMEM((2, seq // TK, LANES, TK), bf16),
            pltpu.VMEM((seq // TQ, LANES, TQ), f32),
        ],
        compiler_params=pltpu.CompilerParams(
            dimension_semantics=("arbitrary", "arbitrary"), vmem_limit_bytes=VMEM_LIMIT),
    )(qkv3, qkv3, qkv3, go, ntri, tail)


def _mlp_kernel(x_ref, sb_ref, sg_ref, wo_ref, g2_ref, w1_ref, w2_ref, o_ref, *, ff_chunk):
    mix = jnp.concatenate([sb_ref[...], sg_ref[...]], axis=1)
    h1 = x_ref[...] + jnp.dot(mix, wo_ref[...], preferred_element_type=f32)
    ms = jnp.mean(h1 * h1, axis=-1, keepdims=True)
    hn = (h1 * lax.rsqrt(ms + EPS) * g2_ref[...]).astype(bf16)
    ff = jnp.zeros_like(h1)
    for j in range(D_FF // ff_chunk):
        cols = slice(j * ff_chunk, (j + 1) * ff_chunk)
        a = jnp.maximum(jnp.dot(hn, w1_ref[:, cols], preferred_element_type=f32), 0.0)
        ff = ff + jnp.dot((a * a).astype(bf16), w2_ref[cols, :], preferred_element_type=f32)
    o_ref[...] = h1 + ff


def _mlp(x2, o_sb, o_sg, wo_bf, g2, w1_bf, w2_bf, tm, ff_chunk):
    m = x2.shape[0]
    const = lambda shape: pl.BlockSpec(shape, lambda i: (0, 0), pipeline_mode=pl.Buffered(1))
    return pl.pallas_call(
        functools.partial(_mlp_kernel, ff_chunk=ff_chunk),
        name="out_proj_mlp",
        grid=(m // tm,),
        in_specs=[
            pl.BlockSpec((tm, D_MODEL), lambda i: (i, 0)),
            pl.BlockSpec((tm, SB_WIDTH), lambda i: (i, 0)),
            pl.BlockSpec((tm, SG_WIDTH), lambda i: (i, 0)),
            const((D_MODEL, D_MODEL)),
            const((1, D_MODEL)),
            const((D_MODEL, D_FF)),
            const((D_FF, D_MODEL)),
        ],
        out_specs=pl.BlockSpec((tm, D_MODEL), lambda i: (i, 0)),
        out_shape=jax.ShapeDtypeStruct((m, D_MODEL), f32),
        compiler_params=pltpu.CompilerParams(
            dimension_semantics=("arbitrary",), vmem_limit_bytes=VMEM_LIMIT),
    )(x2, o_sb, o_sg, wo_bf, g2, w1_bf, w2_bf)


def kernel(x, norm1_g, w_in, q_norm_g, k_norm_g, sg_v_norm_g, sg_w, sg_b, sb_out_norm_g,
           sg_out_norm_g, w_out, norm2_g, w_ff1, w_ff2):
    b, seq, d = x.shape
    assert d == D_MODEL and seq % TQ == 0 and w_in.shape == (D_MODEL, IN_WIDTH)
    m = b * seq
    x2 = x.reshape(m, d)
    pair_gain = lambda g: jnp.tile(g.astype(f32), 2).reshape(1, LANES)

    bias_tiles = jnp.repeat(sg_b.astype(f32).T, HEAD_DIM, axis=1)
    qkv, o_sg = _in_proj(x2, norm1_g.reshape(1, d), w_in.astype(bf16), pair_gain(q_norm_g),
                         pair_gain(k_norm_g), sg_w, bias_tiles, pair_gain(sg_v_norm_g),
                         pair_gain(sg_out_norm_g), tm=2048, sub_rows=256)

    tri = (jnp.arange(TK)[None, :] >= jnp.arange(TK)[:, None]).astype(f32)
    carry_cols = jnp.broadcast_to((jnp.arange(TK // 2)[None, :] < 3).astype(f32), (TK, TK // 2))
    tail = jnp.stack([jnp.concatenate([tri[:, h * (TK // 2):(h + 1) * (TK // 2)], carry_cols], axis=1)
                      for h in (0, 1)])
    o_sb = _sb_attn(qkv.reshape(b, seq, 3 * SB_WIDTH), pair_gain(sb_out_norm_g),
                    (-tri).astype(bf16), (-tail).astype(bf16))

    out = _mlp(x2, o_sb.reshape(m, SB_WIDTH), o_sg, w_out.astype(bf16),
               norm2_g.reshape(1, d), w_ff1.astype(bf16), w_ff2.astype(bf16), tm=512, ff_chunk=1024)
    return out.reshape(b, seq, d)
```

```python
import functools
import math

import jax
import jax.numpy as jnp
from jax import lax
from jax.experimental import pallas as pl
from jax.experimental.pallas import tpu as pltpu

D_MODEL = 1024
HEAD_DIM = 64
SB_HEADS = 8
SG_GROUPS = 8
SB_WIDTH = SB_HEADS * HEAD_DIM
SG_WIDTH = SG_GROUPS * HEAD_DIM
IN_WIDTH = 3 * SB_WIDTH + 2 * SG_WIDTH
CHUNK = 128
D_FF = 4 * D_MODEL
EPS = 1e-6

LANES = 128
PAIRS = SB_WIDTH // LANES
TQ = 512
TK = 128
KT_PER_QB = TQ // TK
LAG_AB = 2
LAG_BC = 1
NEAR_TILES = 2
NEAR_LANES = 256
UNDERFLOW_LOG2 = 160.0
LOG2E = 1.4426950408889634
VMEM_LIMIT = 56 * 1024 * 1024

f32 = jnp.float32
bf16 = jnp.bfloat16


def _head_rms(t, g, h0):
    sq = t * t
    s0 = jnp.sum(jnp.where(h0, sq, 0.0), axis=-1, keepdims=True)
    s1 = jnp.sum(jnp.where(h0, 0.0, sq), axis=-1, keepdims=True)
    ms = jnp.where(h0, s0, s1) * (1.0 / HEAD_DIM)
    return t * lax.rsqrt(ms + EPS) * g


def _in_proj_kernel(x_ref, g_ref, w_ref, gq_ref, gk_ref, sgw_ref, sgb_ref, gv_ref, gs_ref,
                    qkv_ref, osg_ref, *, sub_rows):
    h0 = lax.broadcasted_iota(jnp.int32, (1, LANES), 1) < HEAD_DIM
    gq = gq_ref[...] * (LOG2E / math.sqrt(HEAD_DIM))
    gk = gk_ref[...]
    gv = gv_ref[...]
    gs = gs_ref[...]
    causal = (lax.broadcasted_iota(jnp.int32, (CHUNK, CHUNK), 1)
              <= lax.broadcasted_iota(jnp.int32, (CHUNK, CHUNK), 0))
    u_off = 3 * SB_WIDTH
    v_off = 3 * SB_WIDTH + SG_WIDTH
    w_cats = [jnp.concatenate([jnp.where(causal, sgw_ref[2 * t], 0.0),
                               jnp.where(causal, sgw_ref[2 * t + 1], 0.0)], axis=1).astype(bf16)
              for t in range(PAIRS)]

    def project(r):
        rows = slice(r * sub_rows, (r + 1) * sub_rows)
        x = x_ref[rows, :]
        ms = jnp.mean(x * x, axis=-1, keepdims=True)
        xn = (x * lax.rsqrt(ms + EPS) * g_ref[...]).astype(bf16)
        proj = jnp.dot(xn, w_ref[...], preferred_element_type=f32)
        for t in range(PAIRS):
            q_cols = slice(t * LANES, (t + 1) * LANES)
            k_cols = slice(SB_WIDTH + t * LANES, SB_WIDTH + (t + 1) * LANES)
            qkv_ref[rows, q_cols] = _head_rms(proj[:, q_cols], gq, h0).astype(bf16)
            qkv_ref[rows, k_cols] = _head_rms(proj[:, k_cols], gk, h0).astype(bf16)
        qkv_ref[rows, 2 * SB_WIDTH:] = proj[:, 2 * SB_WIDTH:3 * SB_WIDTH].astype(bf16)
        gate = []
        for c in range(sub_rows // CHUNK):
            crow = slice(c * CHUNK, (c + 1) * CHUNK)
            for t in range(PAIRS):
                vn = _head_rms(proj[crow, v_off + t * LANES:v_off + (t + 1) * LANES], gv, h0)
                rhs = jnp.concatenate(
                    [jnp.where(h0, vn, 0.0), jnp.where(h0, 0.0, vn)], axis=0).astype(bf16)
                gate.append((r * sub_rows + c * CHUNK, t, rhs,
                             proj[crow, u_off + t * LANES:u_off + (t + 1) * LANES]))
        return gate

    def spatial_gate(gate):
        for row0, t, rhs, u in gate:
            y = (jnp.dot(w_cats[t], rhs, preferred_element_type=f32)
                 + sgb_ref[:, t * LANES:(t + 1) * LANES])
            osg_ref[row0:row0 + CHUNK, t * LANES:(t + 1) * LANES] = (
                _head_rms(u * y, gs, h0).astype(bf16))

    pending = None
    for r in range(x_ref.shape[0] // sub_rows):
        gate = project(r)
        if pending is not None:
            spatial_gate(pending)
        pending = gate
    spatial_gate(pending)


def _in_proj(x2, g, w_bf, gq, gk, sg_w, bias_tiles, gv, gs, tm, sub_rows):
    m = x2.shape[0]
    vec = pl.BlockSpec((1, LANES), lambda i: (0, 0))
    return pl.pallas_call(
        functools.partial(_in_proj_kernel, sub_rows=sub_rows),
        name="in_proj",
        grid=(m // tm,),
        in_specs=[
            pl.BlockSpec((tm, D_MODEL), lambda i: (i, 0)),
            pl.BlockSpec((1, D_MODEL), lambda i: (0, 0)),
            pl.BlockSpec((D_MODEL, IN_WIDTH), lambda i: (0, 0), pipeline_mode=pl.Buffered(1)),
            vec, vec,
            pl.BlockSpec((SG_GROUPS, CHUNK, CHUNK), lambda i: (0, 0, 0)),
            pl.BlockSpec((CHUNK, SG_WIDTH), lambda i: (0, 0)),
            vec, vec,
        ],
        out_specs=[pl.BlockSpec((tm, 3 * SB_WIDTH), lambda i: (i, 0)),
                   pl.BlockSpec((tm, SG_WIDTH), lambda i: (i, 0))],
        out_shape=[jax.ShapeDtypeStruct((m, 3 * SB_WIDTH), bf16),
                   jax.ShapeDtypeStruct((m, SG_WIDTH), bf16)],
        compiler_params=pltpu.CompilerParams(
            dimension_semantics=("arbitrary",), vmem_limit_bytes=VMEM_LIMIT),
    )(x2, g, w_bf, gq, gk, sg_w, bias_tiles, gv, gs)


def _sb_attn_kernel(q_ref, k_ref, v_ref, go_ref, ntri_ref, tail_ref, o_ref,
                    qT_s, k2_s, lhs_s, vT_s, acc_s, c_s):
    seq = q_ref.shape[1]
    nqb = seq // TQ
    half_k = TK // 2
    h0 = lax.broadcasted_iota(jnp.int32, (1, LANES), 1) < HEAD_DIM
    sub0 = lax.broadcasted_iota(jnp.int32, (LANES, 1), 0) < HEAD_DIM
    go = go_ref[...]

    for i in range(nqb):
        rows = slice(i * TQ, (i + 1) * TQ)
        qT_s[i] = q_ref[0, rows, :].astype(f32).T.astype(bf16)
        vT = v_ref[0, rows, :].astype(f32).T
        for j in range(TQ // TK):
            vt = vT[:, j * TK:(j + 1) * TK]
            vT_s[0, i * (TQ // TK) + j] = jnp.where(sub0, vt, 0.0).astype(bf16)
            vT_s[1, i * (TQ // TK) + j] = jnp.where(sub0, 0.0, vt).astype(bf16)
    ntri = ntri_ref[...].astype(f32)
    for kt in range(seq // TK):
        k_t = k_ref[0, kt * TK:(kt + 1) * TK, :].astype(f32)
        k2_s[kt] = jnp.concatenate(
            [jnp.where(h0, k_t, 0.0), jnp.where(h0, 0.0, k_t)], axis=0).astype(bf16)
        lhs_s[kt, 0] = jnp.concatenate(
            [jnp.where(h0, k_t, ntri).astype(bf16), tail_ref[0]], axis=1)
        lhs_s[kt, 1] = jnp.concatenate(
            [jnp.where(h0, ntri, k_t).astype(bf16), tail_ref[1]], axis=1)

    def scores(kt, qT, mask):
        z2 = jnp.dot(k2_s[kt], qT, preferred_element_type=f32)
        w = jnp.exp2(-jnp.abs(z2))
        sp = jnp.maximum(z2, 0.0) + jnp.log(1.0 + w) * LOG2E
        if mask is not None:
            sp = jnp.where(mask, sp, 0.0)
        return sp.astype(bf16), (z2[0:1, :], z2[TK:TK + 1, :])

    def carry_rows(c):
        c1 = c.astype(bf16).astype(f32)
        c2 = (c - c1).astype(bf16).astype(f32)
        c3 = c - c1 - c2
        row = lax.broadcasted_iota(jnp.int32, (half_k, 1), 0)
        return jnp.where(row == 0, c1, jnp.where(row == 1, c2, jnp.where(row == 2, c3, 0.0))
                         ).astype(bf16)

    def weights(kt, h, qT, sp, z2_row0, c, mask):
        own = slice(h * half_k, (h + 1) * half_k)
        other = slice((1 - h) * half_k, (2 - h) * half_k)
        blocks = [qT[own], sp[other]] if h == 0 else [sp[other], qT[own]]
        rhs = jnp.concatenate(blocks + [sp[own], carry_rows(c)], axis=0)
        expo = jnp.dot(lhs_s[kt, h], rhs, preferred_element_type=f32)
        a = jnp.exp2(expo)
        if mask is not None:
            a = jnp.where(mask, a, 0.0)
        return a.astype(bf16), z2_row0 - expo[0:1, :]

    def finish_block(qb):
        acc = acc_s[qb]
        sq = acc * acc
        r0 = lax.rsqrt(jnp.sum(sq[:HEAD_DIM], axis=0, keepdims=True) * (1.0 / HEAD_DIM) + EPS)
        r1 = lax.rsqrt(jnp.sum(sq[HEAD_DIM:], axis=0, keepdims=True) * (1.0 / HEAD_DIM) + EPS)
        out = (acc * jnp.where(sub0, r0, r1)).T * go
        o_ref[0, qb * TQ:(qb + 1) * TQ, :] = out.astype(o_ref.dtype)

    def run_tiles(tiles, carry):
        def stage_b(i, sp, z2_rows):
            qb, kt, lo, hi, mask = tiles[i]
            qT = qT_s[qb, :, lo:hi]
            probs = []
            for h in (0, 1):
                c = carry[(qb, h)]
                a, new = weights(kt, h, qT, sp[h * TK:(h + 1) * TK], z2_rows[h], c[:, lo:hi], mask)
                parts = ([c[:, :lo]] if lo else []) + [new] + ([c[:, hi:]] if hi < TQ else [])
                carry[(qb, h)] = jnp.concatenate(parts, axis=1) if len(parts) > 1 else new
                probs.append(a)
            return i, jnp.concatenate(probs, axis=0)

        def stage_c(i, probs):
            qb, kt, lo, hi, mask = tiles[i]
            v_cat = jnp.concatenate([vT_s[0, kt], vT_s[1, kt]], axis=1)
            acc_s[qb, :, lo:hi] += jnp.dot(v_cat, probs, preferred_element_type=f32)

        queue_b, queue_c = [], []
        for i, (qb, kt, lo, hi, mask) in enumerate(tiles):
            both = None if mask is None else jnp.concatenate([mask, mask], axis=0)
            queue_b.append((i,) + scores(kt, qT_s[qb, :, lo:hi], both))
            if len(queue_b) > LAG_AB:
                queue_c.append(stage_b(*queue_b.pop(0)))
            if len(queue_c) > LAG_BC:
                stage_c(*queue_c.pop(0))
        while queue_b:
            queue_c.append(stage_b(*queue_b.pop(0)))
            if len(queue_c) > LAG_BC:
                stage_c(*queue_c.pop(0))
        while queue_c:
            stage_c(*queue_c.pop(0))

    def near_kts(qb):
        return list(reversed(range(max(qb * KT_PER_QB - NEAR_TILES, 0), qb * KT_PER_QB)))

    always = []
    for qb in range(nqb):
        for j in reversed(range(KT_PER_QB)):
            width = TQ - j * TK
            mask = (lax.broadcasted_iota(jnp.int32, (TK, width), 0)
                    < lax.broadcasted_iota(jnp.int32, (TK, width), 1))
            always.append((qb, qb * KT_PER_QB + j, j * TK, TQ, mask))
        always += [(qb, kt, 0, NEAR_LANES, None) for kt in near_kts(qb)]
    for qb in range(nqb):
        acc_s[qb] = jnp.zeros((LANES, TQ), f32)
    carry = {(qb, h): jnp.zeros((1, TQ), f32) for qb in range(nqb) for h in (0, 1)}
    run_tiles(always, carry)
    for (qb, h), c in carry.items():
        c_s[qb, h, 0:1, :] = c

    def guarded(qb, lo, hi, kts):
        c0, c1 = c_s[qb, 0, 0:1, :], c_s[qb, 1, 0:1, :]

        @pl.when(jnp.min(jnp.minimum(c0[:, lo:hi], c1[:, lo:hi])) < UNDERFLOW_LOG2)
        def _():
            local = {(qb, 0): c0, (qb, 1): c1}
            run_tiles([(qb, kt, lo, hi, None) for kt in kts], local)
            c_s[qb, 0, 0:1, :] = local[(qb, 0)]
            c_s[qb, 1, 0:1, :] = local[(qb, 1)]

    for qb in range(nqb):
        if near_kts(qb):
            guarded(qb, NEAR_LANES, TQ, near_kts(qb))
        far = list(reversed(range(max(qb * KT_PER_QB - NEAR_TILES, 0))))
        if far:
            guarded(qb, 0, TQ, far)

    for qb in range(nqb):
        finish_block(qb)


def _sb_attn(qkv3, go, ntri, tail):
    b, seq, _ = qkv3.shape
    blk = lambda off: pl.BlockSpec((1, seq, LANES), lambda i, p: (i, 0, off + p))
    return pl.pallas_call(
        _sb_attn_kernel,
        name="sb_attn",
        grid=(b, PAIRS),
        in_specs=[blk(0), blk(PAIRS), blk(2 * PAIRS),
                  pl.BlockSpec((1, LANES), lambda i, p: (0, 0)),
                  pl.BlockSpec((TK, TK), lambda i, p: (0, 0)),
                  pl.BlockSpec((2, TK, TK), lambda i, p: (0, 0, 0))],
        out_specs=pl.BlockSpec((1, seq, LANES), lambda i, p: (i, 0, p)),
        out_shape=jax.ShapeDtypeStruct((b, seq, SB_WIDTH), bf16),
        scratch_shapes=[
            pltpu.VMEM((seq // TQ, LANES, TQ), bf16),
            pltpu.VMEM((seq // TK, 2 * TK, LANES), bf16),
            pltpu.VMEM((seq // TK, 2, TK, 2 * TK), bf16),
            pltpu.VMEM((2, seq // TK, LANES, TK), bf16),
            pltpu.VMEM((seq // TQ, LANES, TQ), f32),
            pltpu.VMEM((seq // TQ, 2, 8, TQ), f32),
        ],
        compiler_params=pltpu.CompilerParams(
            dimension_semantics=("arbitrary", "arbitrary"), vmem_limit_bytes=VMEM_LIMIT),
    )(qkv3, qkv3, qkv3, go, ntri, tail)


def _mlp_kernel(x_ref, sb_ref, sg_ref, wo_ref, g2_ref, w1_ref, w2_ref, o_ref, *, ff_chunk):
    mix = jnp.concatenate([sb_ref[...], sg_ref[...]], axis=1)
    h1 = x_ref[...] + jnp.dot(mix, wo_ref[...], preferred_element_type=f32)
    ms = jnp.mean(h1 * h1, axis=-1, keepdims=True)
    hn = (h1 * lax.rsqrt(ms + EPS) * g2_ref[...]).astype(bf16)
    ff = jnp.zeros_like(h1)
    for j in range(D_FF // ff_chunk):
        cols = slice(j * ff_chunk, (j + 1) * ff_chunk)
        a = jnp.maximum(jnp.dot(hn, w1_ref[:, cols], preferred_element_type=f32), 0.0)
        ff = ff + jnp.dot((a * a).astype(bf16), w2_ref[cols, :], preferred_element_type=f32)
    o_ref[...] = h1 + ff


def _mlp(x2, o_sb, o_sg, wo_bf, g2, w1_bf, w2_bf, tm, ff_chunk):
    m = x2.shape[0]
    const = lambda shape: pl.BlockSpec(shape, lambda i: (0, 0), pipeline_mode=pl.Buffered(1))
    return pl.pallas_call(
        functools.partial(_mlp_kernel, ff_chunk=ff_chunk),
        name="out_proj_mlp",
        grid=(m // tm,),
        in_specs=[
            pl.BlockSpec((tm, D_MODEL), lambda i: (i, 0)),
            pl.BlockSpec((tm, SB_WIDTH), lambda i: (i, 0)),
            pl.BlockSpec((tm, SG_WIDTH), lambda i: (i, 0)),
            const((D_MODEL, D_MODEL)),
            const((1, D_MODEL)),
            const((D_MODEL, D_FF)),
            const((D_FF, D_MODEL)),
        ],
        out_specs=pl.BlockSpec((tm, D_MODEL), lambda i: (i, 0)),
        out_shape=jax.ShapeDtypeStruct((m, D_MODEL), f32),
        compiler_params=pltpu.CompilerParams(
            dimension_semantics=("arbitrary",), vmem_limit_bytes=VMEM_LIMIT),
    )(x2, o_sb, o_sg, wo_bf, g2, w1_bf, w2_bf)


def kernel(x, norm1_g, w_in, q_norm_g, k_norm_g, sg_v_norm_g, sg_w, sg_b, sb_out_norm_g,
           sg_out_norm_g, w_out, norm2_g, w_ff1, w_ff2):
    b, seq, d = x.shape
    assert d == D_MODEL and seq % TQ == 0 and w_in.shape == (D_MODEL, IN_WIDTH)
    m = b * seq
    x2 = x.reshape(m, d)
    pair_gain = lambda g: jnp.tile(g.astype(f32), 2).reshape(1, LANES)

    bias_tiles = jnp.repeat(sg_b.astype(f32).T, HEAD_DIM, axis=1)
    qkv, o_sg = _in_proj(x2, norm1_g.reshape(1, d), w_in.astype(bf16), pair_gain(q_norm_g),
                         pair_gain(k_norm_g), sg_w, bias_tiles, pair_gain(sg_v_norm_g),
                         pair_gain(sg_out_norm_g), tm=2048, sub_rows=256)

    tri = (jnp.arange(TK)[None, :] >= jnp.arange(TK)[:, None]).astype(f32)
    carry_cols = jnp.broadcast_to((jnp.arange(TK // 2)[None, :] < 3).astype(f32), (TK, TK // 2))
    tail = jnp.stack([jnp.concatenate([tri[:, h * (TK // 2):(h + 1) * (TK // 2)], carry_cols], axis=1)
                      for h in (0, 1)])
    o_sb = _sb_attn(qkv.reshape(b, seq, 3 * SB_WIDTH), pair_gain(sb_out_norm_g),
                    (-tri).astype(bf16), (-tail).astype(bf16))

    out = _mlp(x2, o_sb.reshape(m, SB_WIDTH), o_sg, w_out.astype(bf16),
               norm2_g.reshape(1, d), w_ff1.astype(bf16), w_ff2.astype(bf16), tm=512, ff_chunk=1024)
    return out.reshape(b, seq, d)
```

```python
import functools
import math

import jax
import jax.numpy as jnp
from jax import lax
from jax.experimental import pallas as pl
from jax.experimental.pallas import tpu as pltpu

D_MODEL = 1024
HEAD_DIM = 64
SB_HEADS = 8
SG_GROUPS = 8
SB_WIDTH = SB_HEADS * HEAD_DIM
SG_WIDTH = SG_GROUPS * HEAD_DIM
IN_WIDTH = 3 * SB_WIDTH + 2 * SG_WIDTH
CHUNK = 128
D_FF = 4 * D_MODEL
EPS = 1e-6

LANES = 128
PAIRS = SB_WIDTH // LANES
TQ = 512
TK = 128
KT_PER_QB = TQ // TK
LAG_AB = 2
LAG_BC = 1
NEAR_TILES = 2
NEAR_LANES = 256
UNDERFLOW_LOG2 = 160.0
LOG2E = 1.4426950408889634
VMEM_LIMIT = 56 * 1024 * 1024

f32 = jnp.float32
bf16 = jnp.bfloat16


def _head_rms(t, g, h0):
    sq = t * t
    s0 = jnp.sum(jnp.where(h0, sq, 0.0), axis=-1, keepdims=True)
    s1 = jnp.sum(jnp.where(h0, 0.0, sq), axis=-1, keepdims=True)
    ms = jnp.where(h0, s0, s1) * (1.0 / HEAD_DIM)
    return t * lax.rsqrt(ms + EPS) * g


def _in_proj_kernel(x_ref, g_ref, w_ref, gq_ref, gk_ref, sgw_ref, sgb_ref, gv_ref, gs_ref,
                    wo_ref, w1_ref, w2_ref, qkv_ref, osg_ref, wo_bf_ref, w1_bf_ref, w2_bf_ref,
                    *, sub_rows):
    h0 = lax.broadcasted_iota(jnp.int32, (1, LANES), 1) < HEAD_DIM
    gq = gq_ref[...] * (LOG2E / math.sqrt(HEAD_DIM))
    gk = gk_ref[...]
    gv = gv_ref[...]
    gs = gs_ref[...]
    causal = (lax.broadcasted_iota(jnp.int32, (CHUNK, CHUNK), 1)
              <= lax.broadcasted_iota(jnp.int32, (CHUNK, CHUNK), 0))
    u_off = 3 * SB_WIDTH
    v_off = 3 * SB_WIDTH + SG_WIDTH
    w_cats = [jnp.concatenate([jnp.where(causal, sgw_ref[2 * t], 0.0),
                               jnp.where(causal, sgw_ref[2 * t + 1], 0.0)], axis=1).astype(bf16)
              for t in range(PAIRS)]

    def project(r):
        rows = slice(r * sub_rows, (r + 1) * sub_rows)
        x = x_ref[rows, :]
        ms = jnp.mean(x * x, axis=-1, keepdims=True)
        xn = (x * lax.rsqrt(ms + EPS) * g_ref[...]).astype(bf16)
        proj = jnp.dot(xn, w_ref[...], preferred_element_type=f32)
        for t in range(PAIRS):
            q_cols = slice(t * LANES, (t + 1) * LANES)
            k_cols = slice(SB_WIDTH + t * LANES, SB_WIDTH + (t + 1) * LANES)
            qkv_ref[rows, q_cols] = _head_rms(proj[:, q_cols], gq, h0).astype(bf16)
            qkv_ref[rows, k_cols] = _head_rms(proj[:, k_cols], gk, h0).astype(bf16)
        qkv_ref[rows, 2 * SB_WIDTH:] = proj[:, 2 * SB_WIDTH:3 * SB_WIDTH].astype(bf16)
        gate = []
        for c in range(sub_rows // CHUNK):
            crow = slice(c * CHUNK, (c + 1) * CHUNK)
            for t in range(PAIRS):
                vn = _head_rms(proj[crow, v_off + t * LANES:v_off + (t + 1) * LANES], gv, h0)
                rhs = jnp.concatenate(
                    [jnp.where(h0, vn, 0.0), jnp.where(h0, 0.0, vn)], axis=0).astype(bf16)
                gate.append((r * sub_rows + c * CHUNK, t, rhs,
                             proj[crow, u_off + t * LANES:u_off + (t + 1) * LANES]))
        return gate

    def spatial_gate(gate):
        for row0, t, rhs, u in gate:
            y = (jnp.dot(w_cats[t], rhs, preferred_element_type=f32)
                 + sgb_ref[:, t * LANES:(t + 1) * LANES])
            osg_ref[row0:row0 + CHUNK, t * LANES:(t + 1) * LANES] = (
                _head_rms(u * y, gs, h0).astype(bf16))

    pending = None
    for r in range(x_ref.shape[0] // sub_rows):
        gate = project(r)
        if pending is not None:
            spatial_gate(pending)
        pending = gate
    spatial_gate(pending)

    wo_bf_ref[...] = wo_ref[...].astype(bf16)
    w1_bf_ref[...] = w1_ref[...].astype(bf16)
    w2_bf_ref[...] = w2_ref[...].astype(bf16)


def _in_proj(x2, g, w_bf, gq, gk, sg_w, bias_tiles, gv, gs, w_out, w_ff1, w_ff2, tm, sub_rows):
    m = x2.shape[0]
    steps = m // tm
    vec = pl.BlockSpec((1, LANES), lambda i: (0, 0))
    rows = lambda w: pl.BlockSpec((w.shape[0] // steps, w.shape[1]), lambda i: (i, 0))
    weights = (w_out, w_ff1, w_ff2)
    return pl.pallas_call(
        functools.partial(_in_proj_kernel, sub_rows=sub_rows),
        name="in_proj",
        grid=(m // tm,),
        in_specs=[
            pl.BlockSpec((tm, D_MODEL), lambda i: (i, 0)),
            pl.BlockSpec((1, D_MODEL), lambda i: (0, 0)),
            pl.BlockSpec((D_MODEL, IN_WIDTH), lambda i: (0, 0), pipeline_mode=pl.Buffered(1)),
            vec, vec,
            pl.BlockSpec((SG_GROUPS, CHUNK, CHUNK), lambda i: (0, 0, 0)),
            pl.BlockSpec((CHUNK, SG_WIDTH), lambda i: (0, 0)),
            vec, vec,
        ] + [rows(w) for w in weights],
        out_specs=[pl.BlockSpec((tm, 3 * SB_WIDTH), lambda i: (i, 0)),
                   pl.BlockSpec((tm, SG_WIDTH), lambda i: (i, 0))] + [rows(w) for w in weights],
        out_shape=[jax.ShapeDtypeStruct((m, 3 * SB_WIDTH), bf16),
                   jax.ShapeDtypeStruct((m, SG_WIDTH), bf16)]
        + [jax.ShapeDtypeStruct(w.shape, bf16) for w in weights],
        compiler_params=pltpu.CompilerParams(
            dimension_semantics=("arbitrary",), vmem_limit_bytes=VMEM_LIMIT),
    )(x2, g, w_bf, gq, gk, sg_w, bias_tiles, gv, gs, *weights)


def _sb_attn_kernel(q_ref, k_ref, v_ref, go_ref, ntri_ref, tail_ref, o_ref,
                    qT_s, k2_s, lhs_s, vT_s, acc_s, c_s):
    seq = q_ref.shape[1]
    nqb = seq // TQ
    half_k = TK // 2
    h0 = lax.broadcasted_iota(jnp.int32, (1, LANES), 1) < HEAD_DIM
    sub0 = lax.broadcasted_iota(jnp.int32, (LANES, 1), 0) < HEAD_DIM
    go = go_ref[...]

    for i in range(nqb):
        rows = slice(i * TQ, (i + 1) * TQ)
        qT_s[i] = q_ref[0, rows, :].astype(f32).T.astype(bf16)
        vT = v_ref[0, rows, :].astype(f32).T
        for j in range(TQ // TK):
            vt = vT[:, j * TK:(j + 1) * TK]
            vT_s[0, i * (TQ // TK) + j] = jnp.where(sub0, vt, 0.0).astype(bf16)
            vT_s[1, i * (TQ // TK) + j] = jnp.where(sub0, 0.0, vt).astype(bf16)
    ntri = ntri_ref[...].astype(f32)
    for kt in range(seq // TK):
        k_t = k_ref[0, kt * TK:(kt + 1) * TK, :].astype(f32)
        k2_s[kt] = jnp.concatenate(
            [jnp.where(h0, k_t, 0.0), jnp.where(h0, 0.0, k_t)], axis=0).astype(bf16)
        lhs_s[kt, 0] = jnp.concatenate(
            [jnp.where(h0, k_t, ntri).astype(bf16), tail_ref[0]], axis=1)
        lhs_s[kt, 1] = jnp.concatenate(
            [jnp.where(h0, ntri, k_t).astype(bf16), tail_ref[1]], axis=1)

    def scores(kt, qT, mask):
        z2 = jnp.dot(k2_s[kt], qT, preferred_element_type=f32)
        w = jnp.exp2(-jnp.abs(z2))
        sp = jnp.maximum(z2, 0.0) + jnp.log(1.0 + w) * LOG2E
        if mask is not None:
            sp = jnp.where(mask, sp, 0.0)
        return sp.astype(bf16), (z2[0:1, :], z2[TK:TK + 1, :])

    def carry_rows(c):
        c1 = c.astype(bf16).astype(f32)
        c2 = (c - c1).astype(bf16).astype(f32)
        c3 = c - c1 - c2
        row = lax.broadcasted_iota(jnp.int32, (half_k, 1), 0)
        return jnp.where(row == 0, c1, jnp.where(row == 1, c2, jnp.where(row == 2, c3, 0.0))
                         ).astype(bf16)

    def weights(kt, h, qT, sp, z2_row0, c, mask):
        own = slice(h * half_k, (h + 1) * half_k)
        other = slice((1 - h) * half_k, (2 - h) * half_k)
        blocks = [qT[own], sp[other]] if h == 0 else [sp[other], qT[own]]
        rhs = jnp.concatenate(blocks + [sp[own], carry_rows(c)], axis=0)
        expo = jnp.dot(lhs_s[kt, h], rhs, preferred_element_type=f32)
        a = jnp.exp2(expo)
        if mask is not None:
            a = jnp.where(mask, a, 0.0)
        return a.astype(bf16), z2_row0 - expo[0:1, :]

    def finish_block(qb):
        acc = acc_s[qb]
        sq = acc * acc
        r0 = lax.rsqrt(jnp.sum(sq[:HEAD_DIM], axis=0, keepdims=True) * (1.0 / HEAD_DIM) + EPS)
        r1 = lax.rsqrt(jnp.sum(sq[HEAD_DIM:], axis=0, keepdims=True) * (1.0 / HEAD_DIM) + EPS)
        out = (acc * jnp.where(sub0, r0, r1)).T * go
        o_ref[0, qb * TQ:(qb + 1) * TQ, :] = out.astype(o_ref.dtype)

    def run_tiles(tiles, carry):
        def stage_b(i, sp, z2_rows):
            qb, kt, lo, hi, mask = tiles[i]
            qT = qT_s[qb, :, lo:hi]
            probs = []
            for h in (0, 1):
                c = carry[(qb, h)]
                a, new = weights(kt, h, qT, sp[h * TK:(h + 1) * TK], z2_rows[h], c[:, lo:hi], mask)
                parts = ([c[:, :lo]] if lo else []) + [new] + ([c[:, hi:]] if hi < TQ else [])
                carry[(qb, h)] = jnp.concatenate(parts, axis=1) if len(parts) > 1 else new
                probs.append(a)
            return i, jnp.concatenate(probs, axis=0)

        def stage_c(i, probs):
            qb, kt, lo, hi, mask = tiles[i]
            v_cat = jnp.concatenate([vT_s[0, kt], vT_s[1, kt]], axis=1)
            acc_s[qb, :, lo:hi] += jnp.dot(v_cat, probs, preferred_element_type=f32)

        queue_b, queue_c = [], []
        for i, (qb, kt, lo, hi, mask) in enumerate(tiles):
            both = None if mask is None else jnp.concatenate([mask, mask], axis=0)
            queue_b.append((i,) + scores(kt, qT_s[qb, :, lo:hi], both))
            if len(queue_b) > LAG_AB:
                queue_c.append(stage_b(*queue_b.pop(0)))
            if len(queue_c) > LAG_BC:
                stage_c(*queue_c.pop(0))
        while queue_b:
            queue_c.append(stage_b(*queue_b.pop(0)))
            if len(queue_c) > LAG_BC:
                stage_c(*queue_c.pop(0))
        while queue_c:
            stage_c(*queue_c.pop(0))

    def near_kts(qb):
        return list(reversed(range(max(qb * KT_PER_QB - NEAR_TILES, 0), qb * KT_PER_QB)))

    always = []
    for qb in range(nqb):
        for j in reversed(range(KT_PER_QB)):
            width = TQ - j * TK
            mask = (lax.broadcasted_iota(jnp.int32, (TK, width), 0)
                    < lax.broadcasted_iota(jnp.int32, (TK, width), 1))
            always.append((qb, qb * KT_PER_QB + j, j * TK, TQ, mask))
        always += [(qb, kt, 0, NEAR_LANES, None) for kt in near_kts(qb)]
    for qb in range(nqb):
        acc_s[qb] = jnp.zeros((LANES, TQ), f32)
    carry = {(qb, h): jnp.zeros((1, TQ), f32) for qb in range(nqb) for h in (0, 1)}
    run_tiles(always, carry)
    for (qb, h), c in carry.items():
        c_s[qb, h, 0:1, :] = c

    def guarded(qb, lo, hi, kts):
        c0, c1 = c_s[qb, 0, 0:1, :], c_s[qb, 1, 0:1, :]

        @pl.when(jnp.min(jnp.minimum(c0[:, lo:hi], c1[:, lo:hi])) < UNDERFLOW_LOG2)
        def _():
            local = {(qb, 0): c0, (qb, 1): c1}
            run_tiles([(qb, kt, lo, hi, None) for kt in kts], local)
            c_s[qb, 0, 0:1, :] = local[(qb, 0)]
            c_s[qb, 1, 0:1, :] = local[(qb, 1)]

    for qb in range(nqb):
        if near_kts(qb):
            guarded(qb, NEAR_LANES, TQ, near_kts(qb))
        far = list(reversed(range(max(qb * KT_PER_QB - NEAR_TILES, 0))))
        if far:
            guarded(qb, 0, TQ, far)

    for qb in range(nqb):
        finish_block(qb)


def _sb_attn(qkv3, go, ntri, tail):
    b, seq, _ = qkv3.shape
    blk = lambda off: pl.BlockSpec((1, seq, LANES), lambda i, p: (i, 0, off + p))
    return pl.pallas_call(
        _sb_attn_kernel,
        name="sb_attn",
        grid=(b, PAIRS),
        in_specs=[blk(0), blk(PAIRS), blk(2 * PAIRS),
                  pl.BlockSpec((1, LANES), lambda i, p: (0, 0)),
                  pl.BlockSpec((TK, TK), lambda i, p: (0, 0)),
                  pl.BlockSpec((2, TK, TK), lambda i, p: (0, 0, 0))],
        out_specs=pl.BlockSpec((1, seq, LANES), lambda i, p: (i, 0, p)),
        out_shape=jax.ShapeDtypeStruct((b, seq, SB_WIDTH), bf16),
        scratch_shapes=[
            pltpu.VMEM((seq // TQ, LANES, TQ), bf16),
            pltpu.VMEM((seq // TK, 2 * TK, LANES), bf16),
            pltpu.VMEM((seq // TK, 2, TK, 2 * TK), bf16),
            pltpu.VMEM((2, seq // TK, LANES, TK), bf16),
            pltpu.VMEM((seq // TQ, LANES, TQ), f32),
            pltpu.VMEM((seq // TQ, 2, 8, TQ), f32),
        ],
        compiler_params=pltpu.CompilerParams(
            dimension_semantics=("arbitrary", "arbitrary"), vmem_limit_bytes=VMEM_LIMIT),
    )(qkv3, qkv3, qkv3, go, ntri, tail)


def _mlp_kernel(x_ref, sb_ref, sg_ref, wo_ref, g2_ref, w1_ref, w2_ref, o_ref, *, ff_chunk):
    mix = jnp.concatenate([sb_ref[...], sg_ref[...]], axis=1)
    h1 = x_ref[...] + jnp.dot(mix, wo_ref[...], preferred_element_type=f32)
    ms = jnp.mean(h1 * h1, axis=-1, keepdims=True)
    hn = (h1 * lax.rsqrt(ms + EPS) * g2_ref[...]).astype(bf16)
    ff = jnp.zeros_like(h1)
    for j in range(D_FF // ff_chunk):
        cols = slice(j * ff_chunk, (j + 1) * ff_chunk)
        a = jnp.maximum(jnp.dot(hn, w1_ref[:, cols], preferred_element_type=f32), 0.0)
        ff = ff + jnp.dot((a * a).astype(bf16), w2_ref[cols, :], preferred_element_type=f32)
    o_ref[...] = h1 + ff


def _mlp(x2, o_sb, o_sg, wo_bf, g2, w1_bf, w2_bf, tm, ff_chunk):
    m = x2.shape[0]
    const = lambda shape: pl.BlockSpec(shape, lambda i: (0, 0), pipeline_mode=pl.Buffered(1))
    return pl.pallas_call(
        functools.partial(_mlp_kernel, ff_chunk=ff_chunk),
        name="out_proj_mlp",
        grid=(m // tm,),
        in_specs=[
            pl.BlockSpec((tm, D_MODEL), lambda i: (i, 0)),
            pl.BlockSpec((tm, SB_WIDTH), lambda i: (i, 0)),
            pl.BlockSpec((tm, SG_WIDTH), lambda i: (i, 0)),
            const((D_MODEL, D_MODEL)),
            const((1, D_MODEL)),
            const((D_MODEL, D_FF)),
            const((D_FF, D_MODEL)),
        ],
        out_specs=pl.BlockSpec((tm, D_MODEL), lambda i: (i, 0)),
        out_shape=jax.ShapeDtypeStruct((m, D_MODEL), f32),
        compiler_params=pltpu.CompilerParams(
            dimension_semantics=("arbitrary",), vmem_limit_bytes=VMEM_LIMIT),
    )(x2, o_sb, o_sg, wo_bf, g2, w1_bf, w2_bf)


def kernel(x, norm1_g, w_in, q_norm_g, k_norm_g, sg_v_norm_g, sg_w, sg_b, sb_out_norm_g,
           sg_out_norm_g, w_out, norm2_g, w_ff1, w_ff2):
    b, seq, d = x.shape
    assert d == D_MODEL and seq % TQ == 0 and w_in.shape == (D_MODEL, IN_WIDTH)
    m = b * seq
    x2 = x.reshape(m, d)
    pair_gain = lambda g: jnp.tile(g.astype(f32), 2).reshape(1, LANES)

    bias_tiles = jnp.repeat(sg_b.astype(f32).T, HEAD_DIM, axis=1)
    qkv, o_sg, wo_bf, w1_bf, w2_bf = _in_proj(
        x2, norm1_g.reshape(1, d), w_in.astype(bf16), pair_gain(q_norm_g), pair_gain(k_norm_g),
        sg_w, bias_tiles, pair_gain(sg_v_norm_g), pair_gain(sg_out_norm_g), w_out, w_ff1, w_ff2,
        tm=2048, sub_rows=256)

    tri = (jnp.arange(TK)[None, :] >= jnp.arange(TK)[:, None]).astype(f32)
    carry_cols = jnp.broadcast_to((jnp.arange(TK // 2)[None, :] < 3).astype(f32), (TK, TK // 2))
    tail = jnp.stack([jnp.concatenate([tri[:, h * (TK // 2):(h + 1) * (TK // 2)], carry_cols], axis=1)
                      for h in (0, 1)])
    o_sb = _sb_attn(qkv.reshape(b, seq, 3 * SB_WIDTH), pair_gain(sb_out_norm_g),
                    (-tri).astype(bf16), (-tail).astype(bf16))

    out = _mlp(x2, o_sb.reshape(m, SB_WIDTH), o_sg, wo_bf, norm2_g.reshape(1, d), w1_bf, w2_bf,
               tm=512, ff_chunk=1024)
    return out.reshape(b, seq, d)
```

```python
import functools
import math

import jax
import jax.numpy as jnp
from jax import lax
from jax.experimental import pallas as pl
from jax.experimental.pallas import tpu as pltpu

D_MODEL = 1024
HEAD_DIM = 64
SB_HEADS = 8
SG_GROUPS = 8
SB_WIDTH = SB_HEADS * HEAD_DIM
SG_WIDTH = SG_GROUPS * HEAD_DIM
IN_WIDTH = 3 * SB_WIDTH + 2 * SG_WIDTH
CHUNK = 128
D_FF = 4 * D_MODEL
EPS = 1e-6

LANES = 128
PAIRS = SB_WIDTH // LANES
TQ = 512
TK = 128
KT_PER_QB = TQ // TK
LAG_AB = 2
LAG_BC = 1
NEAR_TILES = 2
NEAR_LANES = 256
UNDERFLOW_LOG2 = 160.0
LOG2E = 1.4426950408889634
VMEM_LIMIT = 56 * 1024 * 1024

f32 = jnp.float32
bf16 = jnp.bfloat16


def _head_rms(t, g, h0):
    sq = t * t
    s0 = jnp.sum(jnp.where(h0, sq, 0.0), axis=-1, keepdims=True)
    s1 = jnp.sum(jnp.where(h0, 0.0, sq), axis=-1, keepdims=True)
    ms = jnp.where(h0, s0, s1) * (1.0 / HEAD_DIM)
    return t * lax.rsqrt(ms + EPS) * g


def _in_proj_kernel(x_ref, g_ref, w_ref, gq_ref, gk_ref, sgw_ref, sgb_ref, gv_ref, gs_ref,
                    wo_ref, w1_ref, w2_ref, qkv_ref, osg_ref, wo_bf_ref, w1_bf_ref, w2_bf_ref,
                    *, sub_rows):
    h0 = lax.broadcasted_iota(jnp.int32, (1, LANES), 1) < HEAD_DIM
    gq = gq_ref[...] * (LOG2E / math.sqrt(HEAD_DIM))
    gk = gk_ref[...]
    gv = gv_ref[...]
    gs = gs_ref[...]
    causal = (lax.broadcasted_iota(jnp.int32, (CHUNK, CHUNK), 1)
              <= lax.broadcasted_iota(jnp.int32, (CHUNK, CHUNK), 0))
    u_off = 3 * SB_WIDTH
    v_off = 3 * SB_WIDTH + SG_WIDTH
    w_cats = [jnp.concatenate([jnp.where(causal, sgw_ref[2 * t], 0.0),
                               jnp.where(causal, sgw_ref[2 * t + 1], 0.0)], axis=1).astype(bf16)
              for t in range(PAIRS)]

    def project(r):
        rows = slice(r * sub_rows, (r + 1) * sub_rows)
        x = x_ref[rows, :]
        ms = jnp.mean(x * x, axis=-1, keepdims=True)
        xn = (x * lax.rsqrt(ms + EPS) * g_ref[...]).astype(bf16)
        proj = jnp.dot(xn, w_ref[...], preferred_element_type=f32)
        for t in range(PAIRS):
            q_cols = slice(t * LANES, (t + 1) * LANES)
            k_cols = slice(SB_WIDTH + t * LANES, SB_WIDTH + (t + 1) * LANES)
            qkv_ref[rows, q_cols] = _head_rms(proj[:, q_cols], gq, h0).astype(bf16)
            qkv_ref[rows, k_cols] = _head_rms(proj[:, k_cols], gk, h0).astype(bf16)
        qkv_ref[rows, 2 * SB_WIDTH:] = proj[:, 2 * SB_WIDTH:3 * SB_WIDTH].astype(bf16)
        gate = []
        for c in range(sub_rows // CHUNK):
            crow = slice(c * CHUNK, (c + 1) * CHUNK)
            for t in range(PAIRS):
                vn = _head_rms(proj[crow, v_off + t * LANES:v_off + (t + 1) * LANES], gv, h0)
                rhs = jnp.concatenate(
                    [jnp.where(h0, vn, 0.0), jnp.where(h0, 0.0, vn)], axis=0).astype(bf16)
                gate.append((r * sub_rows + c * CHUNK, t, rhs,
                             proj[crow, u_off + t * LANES:u_off + (t + 1) * LANES]))
        return gate

    def spatial_gate(gate):
        for row0, t, rhs, u in gate:
            y = (jnp.dot(w_cats[t], rhs, preferred_element_type=f32)
                 + sgb_ref[:, t * LANES:(t + 1) * LANES])
            osg_ref[row0:row0 + CHUNK, t * LANES:(t + 1) * LANES] = (
                _head_rms(u * y, gs, h0).astype(bf16))

    pending = None
    for r in range(x_ref.shape[0] // sub_rows):
        gate = project(r)
        if pending is not None:
            spatial_gate(pending)
        pending = gate
    spatial_gate(pending)

    wo_bf_ref[...] = wo_ref[...].astype(bf16)
    w1_bf_ref[...] = w1_ref[...].astype(bf16)
    w2_bf_ref[...] = w2_ref[...].astype(bf16)


def _in_proj(x2, g, w_bf, gq, gk, sg_w, bias_tiles, gv, gs, w_out, w_ff1, w_ff2, tm, sub_rows):
    m = x2.shape[0]
    steps = m // tm
    vec = pl.BlockSpec((1, LANES), lambda i: (0, 0))
    rows = lambda w: pl.BlockSpec((w.shape[0] // steps, w.shape[1]), lambda i: (i, 0))
    weights = (w_out, w_ff1, w_ff2)
    return pl.pallas_call(
        functools.partial(_in_proj_kernel, sub_rows=sub_rows),
        name="in_proj",
        grid=(m // tm,),
        in_specs=[
            pl.BlockSpec((tm, D_MODEL), lambda i: (i, 0)),
            pl.BlockSpec((1, D_MODEL), lambda i: (0, 0)),
            pl.BlockSpec((D_MODEL, IN_WIDTH), lambda i: (0, 0), pipeline_mode=pl.Buffered(1)),
            vec, vec,
            pl.BlockSpec((SG_GROUPS, CHUNK, CHUNK), lambda i: (0, 0, 0)),
            pl.BlockSpec((CHUNK, SG_WIDTH), lambda i: (0, 0)),
            vec, vec,
        ] + [rows(w) for w in weights],
        out_specs=[pl.BlockSpec((tm, 3 * SB_WIDTH), lambda i: (i, 0)),
                   pl.BlockSpec((tm, SG_WIDTH), lambda i: (i, 0))] + [rows(w) for w in weights],
        out_shape=[jax.ShapeDtypeStruct((m, 3 * SB_WIDTH), bf16),
                   jax.ShapeDtypeStruct((m, SG_WIDTH), bf16)]
        + [jax.ShapeDtypeStruct(w.shape, bf16) for w in weights],
        compiler_params=pltpu.CompilerParams(
            dimension_semantics=("arbitrary",), vmem_limit_bytes=VMEM_LIMIT),
    )(x2, g, w_bf, gq, gk, sg_w, bias_tiles, gv, gs, *weights)


def _sb_attn_kernel(q_ref, k_ref, v_ref, go_ref, ntri_ref, tail_ref, o_ref,
                    qT_s, k2_s, lhs_s, vT_s, acc_s, c_s):
    seq = q_ref.shape[1]
    nqb = seq // TQ
    half_k = TK // 2
    h0 = lax.broadcasted_iota(jnp.int32, (1, LANES), 1) < HEAD_DIM
    sub0 = lax.broadcasted_iota(jnp.int32, (LANES, 1), 0) < HEAD_DIM
    go = go_ref[...]

    for i in range(nqb):
        rows = slice(i * TQ, (i + 1) * TQ)
        qT_s[i] = q_ref[0, rows, :].astype(f32).T.astype(bf16)
        vT = v_ref[0, rows, :].astype(f32).T
        for j in range(TQ // TK):
            vt = vT[:, j * TK:(j + 1) * TK]
            vT_s[0, i * (TQ // TK) + j] = jnp.where(sub0, vt, 0.0).astype(bf16)
            vT_s[1, i * (TQ // TK) + j] = jnp.where(sub0, 0.0, vt).astype(bf16)
    ntri = ntri_ref[...].astype(f32)
    for kt in range(seq // TK):
        k_t = k_ref[0, kt * TK:(kt + 1) * TK, :].astype(f32)
        k2_s[kt] = jnp.concatenate(
            [jnp.where(h0, k_t, 0.0), jnp.where(h0, 0.0, k_t)], axis=0).astype(bf16)
        lhs_s[kt, 0] = jnp.concatenate(
            [jnp.where(h0, k_t, ntri).astype(bf16), tail_ref[0]], axis=1)
        lhs_s[kt, 1] = jnp.concatenate(
            [jnp.where(h0, ntri, k_t).astype(bf16), tail_ref[1]], axis=1)

    def scores(kt, qT, mask):
        z2 = jnp.dot(k2_s[kt], qT, preferred_element_type=f32)
        w = jnp.exp2(-jnp.abs(z2))
        sp = jnp.maximum(z2, 0.0) + jnp.log(1.0 + w) * LOG2E
        if mask is not None:
            sp = jnp.where(mask, sp, 0.0)
        return sp.astype(bf16), (z2[0:1, :], z2[TK:TK + 1, :])

    def carry_rows(c):
        c1 = c.astype(bf16).astype(f32)
        c2 = (c - c1).astype(bf16).astype(f32)
        c3 = c - c1 - c2
        row = lax.broadcasted_iota(jnp.int32, (half_k, 1), 0)
        return jnp.where(row == 0, c1, jnp.where(row == 1, c2, jnp.where(row == 2, c3, 0.0))
                         ).astype(bf16)

    def weights(kt, h, qT, sp, z2_row0, c, mask):
        own = slice(h * half_k, (h + 1) * half_k)
        other = slice((1 - h) * half_k, (2 - h) * half_k)
        blocks = [qT[own], sp[other]] if h == 0 else [sp[other], qT[own]]
        rhs = jnp.concatenate(blocks + [sp[own], carry_rows(c)], axis=0)
        expo = jnp.dot(lhs_s[kt, h], rhs, preferred_element_type=f32)
        a = jnp.exp2(expo)
        if mask is not None:
            a = jnp.where(mask, a, 0.0)
        return a.astype(bf16), z2_row0 - expo[0:1, :]

    def finish_block(qb):
        acc = acc_s[qb]
        sq = acc * acc
        r0 = lax.rsqrt(jnp.sum(sq[:HEAD_DIM], axis=0, keepdims=True) * (1.0 / HEAD_DIM) + EPS)
        r1 = lax.rsqrt(jnp.sum(sq[HEAD_DIM:], axis=0, keepdims=True) * (1.0 / HEAD_DIM) + EPS)
        out = (acc * jnp.where(sub0, r0, r1)).T * go
        o_ref[0, qb * TQ:(qb + 1) * TQ, :] = out.astype(o_ref.dtype)

    def run_tiles(tiles, carry):
        def stage_b(i, sp, z2_rows):
            qb, kt, lo, hi, mask = tiles[i]
            qT = qT_s[qb, :, lo:hi]
            probs = []
            for h in (0, 1):
                c = carry[(qb, h)]
                a, new = weights(kt, h, qT, sp[h * TK:(h + 1) * TK], z2_rows[h], c[:, lo:hi], mask)
                parts = ([c[:, :lo]] if lo else []) + [new] + ([c[:, hi:]] if hi < TQ else [])
                carry[(qb, h)] = jnp.concatenate(parts, axis=1) if len(parts) > 1 else new
                probs.append(a)
            return i, jnp.concatenate(probs, axis=0)

        def stage_c(i, probs):
            qb, kt, lo, hi, mask = tiles[i]
            v_cat = jnp.concatenate([vT_s[0, kt], vT_s[1, kt]], axis=1)
            acc_s[qb, :, lo:hi] += jnp.dot(v_cat, probs, preferred_element_type=f32)

        queue_b, queue_c = [], []
        for i, (qb, kt, lo, hi, mask) in enumerate(tiles):
            both = None if mask is None else jnp.concatenate([mask, mask], axis=0)
            queue_b.append((i,) + scores(kt, qT_s[qb, :, lo:hi], both))
            if len(queue_b) > LAG_AB:
                queue_c.append(stage_b(*queue_b.pop(0)))
            if len(queue_c) > LAG_BC:
                stage_c(*queue_c.pop(0))
        while queue_b:
            queue_c.append(stage_b(*queue_b.pop(0)))
            if len(queue_c) > LAG_BC:
                stage_c(*queue_c.pop(0))
        while queue_c:
            stage_c(*queue_c.pop(0))

    def near_kts(qb):
        return list(reversed(range(max(qb * KT_PER_QB - NEAR_TILES, 0), qb * KT_PER_QB)))

    always = []
    for qb in range(nqb):
        for j in reversed(range(KT_PER_QB)):
            width = TQ - j * TK
            mask = (lax.broadcasted_iota(jnp.int32, (TK, width), 0)
                    < lax.broadcasted_iota(jnp.int32, (TK, width), 1))
            always.append((qb, qb * KT_PER_QB + j, j * TK, TQ, mask))
        always += [(qb, kt, 0, NEAR_LANES, None) for kt in near_kts(qb)]
    for qb in range(nqb):
        acc_s[qb] = jnp.zeros((LANES, TQ), f32)
    carry = {(qb, h): jnp.zeros((1, TQ), f32) for qb in range(nqb) for h in (0, 1)}
    run_tiles(always, carry)
    for (qb, h), c in carry.items():
        c_s[qb, h, 0:1, :] = c

    def needed(qb, lo, hi):
        return jnp.min(jnp.minimum(carry[(qb, 0)][:, lo:hi],
                                   carry[(qb, 1)][:, lo:hi])) < UNDERFLOW_LOG2

    def guarded(qb, lo, hi, kts, run):
        @pl.when(run)
        def _():
            local = {(qb, 0): c_s[qb, 0, 0:1, :], (qb, 1): c_s[qb, 1, 0:1, :]}
            run_tiles([(qb, kt, lo, hi, None) for kt in kts], local)
            c_s[qb, 0, 0:1, :] = local[(qb, 0)]
            c_s[qb, 1, 0:1, :] = local[(qb, 1)]

    plan = []
    for qb in range(nqb):
        if near_kts(qb):
            plan.append((qb, NEAR_LANES, TQ, near_kts(qb), needed(qb, NEAR_LANES, TQ)))
        far = list(reversed(range(max(qb * KT_PER_QB - NEAR_TILES, 0))))
        if far:
            plan.append((qb, 0, TQ, far, needed(qb, 0, TQ)))
    for args in plan:
        guarded(*args)

    for qb in range(nqb):
        finish_block(qb)


def _sb_attn(qkv3, go, ntri, tail):
    b, seq, _ = qkv3.shape
    blk = lambda off: pl.BlockSpec((1, seq, LANES), lambda i, p: (i, 0, off + p))
    return pl.pallas_call(
        _sb_attn_kernel,
        name="sb_attn",
        grid=(b, PAIRS),
        in_specs=[blk(0), blk(PAIRS), blk(2 * PAIRS),
                  pl.BlockSpec((1, LANES), lambda i, p: (0, 0)),
                  pl.BlockSpec((TK, TK), lambda i, p: (0, 0)),
                  pl.BlockSpec((2, TK, TK), lambda i, p: (0, 0, 0))],
        out_specs=pl.BlockSpec((1, seq, LANES), lambda i, p: (i, 0, p)),
        out_shape=jax.ShapeDtypeStruct((b, seq, SB_WIDTH), bf16),
        scratch_shapes=[
            pltpu.VMEM((seq // TQ, LANES, TQ), bf16),
            pltpu.VMEM((seq // TK, 2 * TK, LANES), bf16),
            pltpu.VMEM((seq // TK, 2, TK, 2 * TK), bf16),
            pltpu.VMEM((2, seq // TK, LANES, TK), bf16),
            pltpu.VMEM((seq // TQ, LANES, TQ), f32),
            pltpu.VMEM((seq // TQ, 2, 8, TQ), f32),
        ],
        compiler_params=pltpu.CompilerParams(
            dimension_semantics=("arbitrary", "arbitrary"), vmem_limit_bytes=VMEM_LIMIT),
    )(qkv3, qkv3, qkv3, go, ntri, tail)


def _mlp_kernel(x_ref, sb_ref, sg_ref, wo_ref, g2_ref, w1_ref, w2_ref, o_ref, *, ff_chunk):
    mix = jnp.concatenate([sb_ref[...], sg_ref[...]], axis=1)
    h1 = x_ref[...] + jnp.dot(mix, wo_ref[...], preferred_element_type=f32)
    ms = jnp.mean(h1 * h1, axis=-1, keepdims=True)
    hn = (h1 * lax.rsqrt(ms + EPS) * g2_ref[...]).astype(bf16)
    ff = jnp.zeros_like(h1)
    for j in range(D_FF // ff_chunk):
        cols = slice(j * ff_chunk, (j + 1) * ff_chunk)
        a = jnp.maximum(jnp.dot(hn, w1_ref[:, cols], preferred_element_type=f32), 0.0)
        ff = ff + jnp.dot((a * a).astype(bf16), w2_ref[cols, :], preferred_element_type=f32)
    o_ref[...] = h1 + ff


def _mlp(x2, o_sb, o_sg, wo_bf, g2, w1_bf, w2_bf, tm, ff_chunk):
    m = x2.shape[0]
    const = lambda shape: pl.BlockSpec(shape, lambda i: (0, 0), pipeline_mode=pl.Buffered(1))
    return pl.pallas_call(
        functools.partial(_mlp_kernel, ff_chunk=ff_chunk),
        name="out_proj_mlp",
        grid=(m // tm,),
        in_specs=[
            pl.BlockSpec((tm, D_MODEL), lambda i: (i, 0)),
            pl.BlockSpec((tm, SB_WIDTH), lambda i: (i, 0)),
            pl.BlockSpec((tm, SG_WIDTH), lambda i: (i, 0)),
            const((D_MODEL, D_MODEL)),
            const((1, D_MODEL)),
            const((D_MODEL, D_FF)),
            const((D_FF, D_MODEL)),
        ],
        out_specs=pl.BlockSpec((tm, D_MODEL), lambda i: (i, 0)),
        out_shape=jax.ShapeDtypeStruct((m, D_MODEL), f32),
        compiler_params=pltpu.CompilerParams(
            dimension_semantics=("arbitrary",), vmem_limit_bytes=VMEM_LIMIT),
    )(x2, o_sb, o_sg, wo_bf, g2, w1_bf, w2_bf)


def kernel(x, norm1_g, w_in, q_norm_g, k_norm_g, sg_v_norm_g, sg_w, sg_b, sb_out_norm_g,
           sg_out_norm_g, w_out, norm2_g, w_ff1, w_ff2):
    b, seq, d = x.shape
    assert d == D_MODEL and seq % TQ == 0 and w_in.shape == (D_MODEL, IN_WIDTH)
    m = b * seq
    x2 = x.reshape(m, d)
    pair_gain = lambda g: jnp.tile(g.astype(f32), 2).reshape(1, LANES)

    bias_tiles = jnp.repeat(sg_b.astype(f32).T, HEAD_DIM, axis=1)
    qkv, o_sg, wo_bf, w1_bf, w2_bf = _in_proj(
        x2, norm1_g.reshape(1, d), w_in.astype(bf16), pair_gain(q_norm_g), pair_gain(k_norm_g),
        sg_w, bias_tiles, pair_gain(sg_v_norm_g), pair_gain(sg_out_norm_g), w_out, w_ff1, w_ff2,
        tm=2048, sub_rows=256)

    tri = (jnp.arange(TK)[None, :] >= jnp.arange(TK)[:, None]).astype(f32)
    carry_cols = jnp.broadcast_to((jnp.arange(TK // 2)[None, :] < 3).astype(f32), (TK, TK // 2))
    tail = jnp.stack([jnp.concatenate([tri[:, h * (TK // 2):(h + 1) * (TK // 2)], carry_cols], axis=1)
                      for h in (0, 1)])
    o_sb = _sb_attn(qkv.reshape(b, seq, 3 * SB_WIDTH), pair_gain(sb_out_norm_g),
                    (-tri).astype(bf16), (-tail).astype(bf16))

    out = _mlp(x2, o_sb.reshape(m, SB_WIDTH), o_sg, wo_bf, norm2_g.reshape(1, d), w1_bf, w2_bf,
               tm=512, ff_chunk=1024)
    return out.reshape(b, seq, d)
```

```python
import functools
import math

import jax
import jax.numpy as jnp
from jax import lax
from jax.experimental import pallas as pl
from jax.experimental.pallas import tpu as pltpu

D_MODEL = 1024
HEAD_DIM = 64
SB_HEADS = 8
SG_GROUPS = 8
SB_WIDTH = SB_HEADS * HEAD_DIM
SG_WIDTH = SG_GROUPS * HEAD_DIM
IN_WIDTH = 3 * SB_WIDTH + 2 * SG_WIDTH
CHUNK = 128
D_FF = 4 * D_MODEL
EPS = 1e-6

LANES = 128
PAIRS = SB_WIDTH // LANES
TQ = 512
TK = 128
KT_PER_QB = TQ // TK
LAG_AB = 2
LAG_BC = 1
NEAR_TILES = 2
NEAR_LANES = 256
UNDERFLOW_LOG2 = 160.0
LOG2E = 1.4426950408889634
VMEM_LIMIT = 56 * 1024 * 1024

f32 = jnp.float32
bf16 = jnp.bfloat16


def _head_rms(t, g, h0):
    sq = t * t
    s0 = jnp.sum(jnp.where(h0, sq, 0.0), axis=-1, keepdims=True)
    s1 = jnp.sum(jnp.where(h0, 0.0, sq), axis=-1, keepdims=True)
    ms = jnp.where(h0, s0, s1) * (1.0 / HEAD_DIM)
    return t * lax.rsqrt(ms + EPS) * g


def _in_proj_kernel(x_ref, g_ref, w_ref, gq_ref, gk_ref, sgw_ref, sgb_ref, gv_ref, gs_ref,
                    wo_ref, w1_ref, w2_ref, qkv_ref, osg_ref, wo_bf_ref, w1_bf_ref, w2_bf_ref,
                    *, sub_rows):
    h0 = lax.broadcasted_iota(jnp.int32, (1, LANES), 1) < HEAD_DIM
    gq = gq_ref[...] * (LOG2E / math.sqrt(HEAD_DIM))
    gk = gk_ref[...]
    gv = gv_ref[...]
    gs = gs_ref[...]
    causal = (lax.broadcasted_iota(jnp.int32, (CHUNK, CHUNK), 1)
              <= lax.broadcasted_iota(jnp.int32, (CHUNK, CHUNK), 0))
    u_off = 3 * SB_WIDTH
    v_off = 3 * SB_WIDTH + SG_WIDTH
    w_cats = [jnp.concatenate([jnp.where(causal, sgw_ref[2 * t], 0.0),
                               jnp.where(causal, sgw_ref[2 * t + 1], 0.0)], axis=1).astype(bf16)
              for t in range(PAIRS)]

    def project(r):
        rows = slice(r * sub_rows, (r + 1) * sub_rows)
        x = x_ref[rows, :]
        ms = jnp.mean(x * x, axis=-1, keepdims=True)
        xn = (x * lax.rsqrt(ms + EPS) * g_ref[...]).astype(bf16)
        proj = jnp.dot(xn, w_ref[...], preferred_element_type=f32)
        for t in range(PAIRS):
            q_cols = slice(t * LANES, (t + 1) * LANES)
            k_cols = slice(SB_WIDTH + t * LANES, SB_WIDTH + (t + 1) * LANES)
            qkv_ref[rows, q_cols] = _head_rms(proj[:, q_cols], gq, h0).astype(bf16)
            qkv_ref[rows, k_cols] = _head_rms(proj[:, k_cols], gk, h0).astype(bf16)
        qkv_ref[rows, 2 * SB_WIDTH:] = proj[:, 2 * SB_WIDTH:3 * SB_WIDTH].astype(bf16)
        gate = []
        for c in range(sub_rows // CHUNK):
            crow = slice(c * CHUNK, (c + 1) * CHUNK)
            for t in range(PAIRS):
                vn = _head_rms(proj[crow, v_off + t * LANES:v_off + (t + 1) * LANES], gv, h0)
                rhs = jnp.concatenate(
                    [jnp.where(h0, vn, 0.0), jnp.where(h0, 0.0, vn)], axis=0).astype(bf16)
                gate.append((r * sub_rows + c * CHUNK, t, rhs,
                             proj[crow, u_off + t * LANES:u_off + (t + 1) * LANES]))
        return gate

    def spatial_gate(gate):
        for row0, t, rhs, u in gate:
            y = (jnp.dot(w_cats[t], rhs, preferred_element_type=f32)
                 + sgb_ref[:, t * LANES:(t + 1) * LANES])
            osg_ref[row0:row0 + CHUNK, t * LANES:(t + 1) * LANES] = (
                _head_rms(u * y, gs, h0).astype(bf16))

    pending = None
    for r in range(x_ref.shape[0] // sub_rows):
        gate = project(r)
        if pending is not None:
            spatial_gate(pending)
        pending = gate
    spatial_gate(pending)

    wo_bf_ref[...] = wo_ref[...].astype(bf16)
    w1_bf_ref[...] = w1_ref[...].astype(bf16)
    w2_bf_ref[...] = w2_ref[...].astype(bf16)


def _in_proj(x2, g, w_bf, gq, gk, sg_w, bias_tiles, gv, gs, w_out, w_ff1, w_ff2, tm, sub_rows):
    m = x2.shape[0]
    steps = m // tm
    vec = pl.BlockSpec((1, LANES), lambda i: (0, 0))
    rows = lambda w: pl.BlockSpec((w.shape[0] // steps, w.shape[1]), lambda i: (i, 0))
    weights = (w_out, w_ff1, w_ff2)
    return pl.pallas_call(
        functools.partial(_in_proj_kernel, sub_rows=sub_rows),
        name="in_proj",
        grid=(m // tm,),
        in_specs=[
            pl.BlockSpec((tm, D_MODEL), lambda i: (i, 0)),
            pl.BlockSpec((1, D_MODEL), lambda i: (0, 0)),
            pl.BlockSpec((D_MODEL, IN_WIDTH), lambda i: (0, 0), pipeline_mode=pl.Buffered(1)),
            vec, vec,
            pl.BlockSpec((SG_GROUPS, CHUNK, CHUNK), lambda i: (0, 0, 0)),
            pl.BlockSpec((CHUNK, SG_WIDTH), lambda i: (0, 0)),
            vec, vec,
        ] + [rows(w) for w in weights],
        out_specs=[pl.BlockSpec((tm, 3 * SB_WIDTH), lambda i: (i, 0)),
                   pl.BlockSpec((tm, SG_WIDTH), lambda i: (i, 0))] + [rows(w) for w in weights],
        out_shape=[jax.ShapeDtypeStruct((m, 3 * SB_WIDTH), bf16),
                   jax.ShapeDtypeStruct((m, SG_WIDTH), bf16)]
        + [jax.ShapeDtypeStruct(w.shape, bf16) for w in weights],
        compiler_params=pltpu.CompilerParams(
            dimension_semantics=("arbitrary",), vmem_limit_bytes=VMEM_LIMIT),
    )(x2, g, w_bf, gq, gk, sg_w, bias_tiles, gv, gs, *weights)


def _sb_attn_kernel(q_ref, k_ref, v_ref, go_ref, ntri_ref, tail_ref, o_ref,
                    qT_s, k2_s, lhs_s, vT_s, acc_s, c_s):
    seq = q_ref.shape[1]
    nqb = seq // TQ
    half_k = TK // 2
    h0 = lax.broadcasted_iota(jnp.int32, (1, LANES), 1) < HEAD_DIM
    sub0 = lax.broadcasted_iota(jnp.int32, (LANES, 1), 0) < HEAD_DIM
    go = go_ref[...]

    for i in range(nqb):
        rows = slice(i * TQ, (i + 1) * TQ)
        qT_s[i] = q_ref[0, rows, :].astype(f32).T.astype(bf16)
        vT = v_ref[0, rows, :].astype(f32).T
        for j in range(TQ // TK):
            vt = vT[:, j * TK:(j + 1) * TK]
            vT_s[0, i * (TQ // TK) + j] = jnp.where(sub0, vt, 0.0).astype(bf16)
            vT_s[1, i * (TQ // TK) + j] = jnp.where(sub0, 0.0, vt).astype(bf16)
    ntri = ntri_ref[...].astype(f32)
    for kt in range(seq // TK):
        k_t = k_ref[0, kt * TK:(kt + 1) * TK, :].astype(f32)
        k2_s[kt] = jnp.concatenate(
            [jnp.where(h0, k_t, 0.0), jnp.where(h0, 0.0, k_t)], axis=0).astype(bf16)
        lhs_s[kt, 0] = jnp.concatenate(
            [jnp.where(h0, k_t, ntri).astype(bf16), tail_ref[0]], axis=1)
        lhs_s[kt, 1] = jnp.concatenate(
            [jnp.where(h0, ntri, k_t).astype(bf16), tail_ref[1]], axis=1)

    def scores(kt, qT, mask):
        z2 = jnp.dot(k2_s[kt], qT, preferred_element_type=f32)
        w = jnp.exp2(-jnp.abs(z2))
        sp = jnp.maximum(z2, 0.0) + jnp.log(1.0 + w) * LOG2E
        if mask is not None:
            sp = jnp.where(mask, sp, 0.0)
        return sp.astype(bf16), (z2[0:1, :], z2[TK:TK + 1, :])

    def carry_rows(c):
        c1 = c.astype(bf16).astype(f32)
        c2 = (c - c1).astype(bf16).astype(f32)
        c3 = c - c1 - c2
        row = lax.broadcasted_iota(jnp.int32, (half_k, 1), 0)
        return jnp.where(row == 0, c1, jnp.where(row == 1, c2, jnp.where(row == 2, c3, 0.0))
                         ).astype(bf16)

    def weights(kt, h, qT, sp, z2_row0, c, mask):
        own = slice(h * half_k, (h + 1) * half_k)
        other = slice((1 - h) * half_k, (2 - h) * half_k)
        blocks = [qT[own], sp[other]] if h == 0 else [sp[other], qT[own]]
        rhs = jnp.concatenate(blocks + [sp[own], carry_rows(c)], axis=0)
        expo = jnp.dot(lhs_s[kt, h], rhs, preferred_element_type=f32)
        a = jnp.exp2(expo)
        if mask is not None:
            a = jnp.where(mask, a, 0.0)
        return a.astype(bf16), z2_row0 - expo[0:1, :]

    def finish_block(qb):
        acc = acc_s[qb]
        sq = acc * acc
        r0 = lax.rsqrt(jnp.sum(sq[:HEAD_DIM], axis=0, keepdims=True) * (1.0 / HEAD_DIM) + EPS)
        r1 = lax.rsqrt(jnp.sum(sq[HEAD_DIM:], axis=0, keepdims=True) * (1.0 / HEAD_DIM) + EPS)
        out = (acc * jnp.where(sub0, r0, r1)).T * go
        o_ref[0, qb * TQ:(qb + 1) * TQ, :] = out.astype(o_ref.dtype)

    def run_tiles(tiles, carry):
        def stage_b(i, sp, z2_rows):
            qb, kt, lo, hi, mask = tiles[i]
            qT = qT_s[qb, :, lo:hi]
            probs = []
            for h in (0, 1):
                c = carry[(qb, h)]
                a, new = weights(kt, h, qT, sp[h * TK:(h + 1) * TK], z2_rows[h], c[:, lo:hi], mask)
                parts = ([c[:, :lo]] if lo else []) + [new] + ([c[:, hi:]] if hi < TQ else [])
                carry[(qb, h)] = jnp.concatenate(parts, axis=1) if len(parts) > 1 else new
                probs.append(a)
            return i, jnp.concatenate(probs, axis=0)

        def stage_c(i, probs):
            qb, kt, lo, hi, mask = tiles[i]
            v_cat = jnp.concatenate([vT_s[0, kt], vT_s[1, kt]], axis=1)
            acc_s[qb, :, lo:hi] += jnp.dot(v_cat, probs, preferred_element_type=f32)

        queue_b, queue_c = [], []
        for i, (qb, kt, lo, hi, mask) in enumerate(tiles):
            both = None if mask is None else jnp.concatenate([mask, mask], axis=0)
            queue_b.append((i,) + scores(kt, qT_s[qb, :, lo:hi], both))
            if len(queue_b) > LAG_AB:
                queue_c.append(stage_b(*queue_b.pop(0)))
            if len(queue_c) > LAG_BC:
                stage_c(*queue_c.pop(0))
        while queue_b:
            queue_c.append(stage_b(*queue_b.pop(0)))
            if len(queue_c) > LAG_BC:
                stage_c(*queue_c.pop(0))
        while queue_c:
            stage_c(*queue_c.pop(0))

    def near_kts(qb):
        return list(reversed(range(max(qb * KT_PER_QB - NEAR_TILES, 0), qb * KT_PER_QB)))

    always = []
    for qb in range(nqb):
        for j in reversed(range(KT_PER_QB)):
            width = TQ - j * TK
            mask = (lax.broadcasted_iota(jnp.int32, (TK, width), 0)
                    < lax.broadcasted_iota(jnp.int32, (TK, width), 1))
            always.append((qb, qb * KT_PER_QB + j, j * TK, TQ, mask))
        always += [(qb, kt, 0, NEAR_LANES, None) for kt in near_kts(qb)]
    for qb in range(nqb):
        acc_s[qb] = jnp.zeros((LANES, TQ), f32)
    carry = {(qb, h): jnp.zeros((1, TQ), f32) for qb in range(nqb) for h in (0, 1)}
    run_tiles(always, carry)
    for (qb, h), c in carry.items():
        c_s[qb, h, 0:1, :] = c

    def needed(qb, lo, hi):
        return jnp.min(jnp.minimum(carry[(qb, 0)][:, lo:hi],
                                   carry[(qb, 1)][:, lo:hi])) < UNDERFLOW_LOG2

    def guarded(qb, lo, hi, kts, run):
        @pl.when(run)
        def _():
            local = {(qb, 0): c_s[qb, 0, 0:1, :], (qb, 1): c_s[qb, 1, 0:1, :]}
            run_tiles([(qb, kt, lo, hi, None) for kt in kts], local)
            c_s[qb, 0, 0:1, :] = local[(qb, 0)]
            c_s[qb, 1, 0:1, :] = local[(qb, 1)]

    plan = []
    for qb in range(nqb):
        if near_kts(qb):
            plan.append((qb, NEAR_LANES, TQ, near_kts(qb), needed(qb, NEAR_LANES, TQ)))
        far = list(reversed(range(max(qb * KT_PER_QB - NEAR_TILES, 0))))
        if far:
            plan.append((qb, 0, TQ, far, needed(qb, 0, TQ)))
    for args in plan:
        guarded(*args)

    for qb in range(nqb):
        finish_block(qb)


def _sb_attn(qkv3, go, ntri, tail):
    b, seq, _ = qkv3.shape
    blk = lambda off: pl.BlockSpec((1, seq, LANES), lambda i, p: (i, 0, off + p))
    return pl.pallas_call(
        _sb_attn_kernel,
        name="sb_attn",
        grid=(b, PAIRS),
        in_specs=[blk(0), blk(PAIRS), blk(2 * PAIRS),
                  pl.BlockSpec((1, LANES), lambda i, p: (0, 0)),
                  pl.BlockSpec((TK, TK), lambda i, p: (0, 0)),
                  pl.BlockSpec((2, TK, TK), lambda i, p: (0, 0, 0))],
        out_specs=pl.BlockSpec((1, seq, LANES), lambda i, p: (i, 0, p)),
        out_shape=jax.ShapeDtypeStruct((b, seq, SB_WIDTH), bf16),
        scratch_shapes=[
            pltpu.VMEM((seq // TQ, LANES, TQ), bf16),
            pltpu.VMEM((seq // TK, 2 * TK, LANES), bf16),
            pltpu.VMEM((seq // TK, 2, TK, 2 * TK), bf16),
            pltpu.VMEM((2, seq // TK, LANES, TK), bf16),
            pltpu.VMEM((seq // TQ, LANES, TQ), f32),
            pltpu.VMEM((seq // TQ, 2, 8, TQ), f32),
        ],
        compiler_params=pltpu.CompilerParams(
            dimension_semantics=("arbitrary", "arbitrary"), vmem_limit_bytes=VMEM_LIMIT),
    )(qkv3, qkv3, qkv3, go, ntri, tail)


def _mlp_kernel(x_ref, sb_ref, sg_ref, wo_ref, g2_ref, w1_ref, w2_ref, o_ref, *, sub_rows, ff_chunk):
    for r in range(x_ref.shape[0] // sub_rows):
        rows = slice(r * sub_rows, (r + 1) * sub_rows)
        mix = jnp.concatenate([sb_ref[rows, :], sg_ref[rows, :]], axis=1)
        h1 = x_ref[rows, :] + jnp.dot(mix, wo_ref[...], preferred_element_type=f32)
        ms = jnp.mean(h1 * h1, axis=-1, keepdims=True)
        hn = (h1 * lax.rsqrt(ms + EPS) * g2_ref[...]).astype(bf16)
        ff = jnp.zeros_like(h1)
        for j in range(D_FF // ff_chunk):
            cols = slice(j * ff_chunk, (j + 1) * ff_chunk)
            a = jnp.maximum(jnp.dot(hn, w1_ref[:, cols], preferred_element_type=f32), 0.0)
            ff = ff + jnp.dot((a * a).astype(bf16), w2_ref[cols, :], preferred_element_type=f32)
        o_ref[rows, :] = h1 + ff


def _mlp(x2, o_sb, o_sg, wo_bf, g2, w1_bf, w2_bf, tm, sub_rows, ff_chunk):
    m = x2.shape[0]
    const = lambda shape: pl.BlockSpec(shape, lambda i: (0, 0), pipeline_mode=pl.Buffered(1))
    return pl.pallas_call(
        functools.partial(_mlp_kernel, sub_rows=sub_rows, ff_chunk=ff_chunk),
        name="out_proj_mlp",
        grid=(m // tm,),
        in_specs=[
            pl.BlockSpec((tm, D_MODEL), lambda i: (i, 0)),
            pl.BlockSpec((tm, SB_WIDTH), lambda i: (i, 0)),
            pl.BlockSpec((tm, SG_WIDTH), lambda i: (i, 0)),
            const((D_MODEL, D_MODEL)),
            const((1, D_MODEL)),
            const((D_MODEL, D_FF)),
            const((D_FF, D_MODEL)),
        ],
        out_specs=pl.BlockSpec((tm, D_MODEL), lambda i: (i, 0)),
        out_shape=jax.ShapeDtypeStruct((m, D_MODEL), f32),
        compiler_params=pltpu.CompilerParams(
            dimension_semantics=("arbitrary",), vmem_limit_bytes=VMEM_LIMIT),
    )(x2, o_sb, o_sg, wo_bf, g2, w1_bf, w2_bf)


def kernel(x, norm1_g, w_in, q_norm_g, k_norm_g, sg_v_norm_g, sg_w, sg_b, sb_out_norm_g,
           sg_out_norm_g, w_out, norm2_g, w_ff1, w_ff2):
    b, seq, d = x.shape
    assert d == D_MODEL and seq % TQ == 0 and w_in.shape == (D_MODEL, IN_WIDTH)
    m = b * seq
    x2 = x.reshape(m, d)
    pair_gain = lambda g: jnp.tile(g.astype(f32), 2).reshape(1, LANES)

    bias_tiles = jnp.repeat(sg_b.astype(f32).T, HEAD_DIM, axis=1)
    qkv, o_sg, wo_bf, w1_bf, w2_bf = _in_proj(
        x2, norm1_g.reshape(1, d), w_in.astype(bf16), pair_gain(q_norm_g), pair_gain(k_norm_g),
        sg_w, bias_tiles, pair_gain(sg_v_norm_g), pair_gain(sg_out_norm_g), w_out, w_ff1, w_ff2,
        tm=2048, sub_rows=256)

    tri = (jnp.arange(TK)[None, :] >= jnp.arange(TK)[:, None]).astype(f32)
    carry_cols = jnp.broadcast_to((jnp.arange(TK // 2)[None, :] < 3).astype(f32), (TK, TK // 2))
    tail = jnp.stack([jnp.concatenate([tri[:, h * (TK // 2):(h + 1) * (TK // 2)], carry_cols], axis=1)
                      for h in (0, 1)])
    o_sb = _sb_attn(qkv.reshape(b, seq, 3 * SB_WIDTH), pair_gain(sb_out_norm_g),
                    (-tri).astype(bf16), (-tail).astype(bf16))

    out = _mlp(x2, o_sb.reshape(m, SB_WIDTH), o_sg, wo_bf, norm2_g.reshape(1, d), w1_bf, w2_bf,
               tm=1024, sub_rows=512, ff_chunk=1024)
    return out.reshape(b, seq, d)
```
